```python
import jax, jax.numpy as jnp
from jax import lax
import numpy as np

D_MODEL = 1024
BATCH = 4
SEQ = 4096
DEPTH = 4
DEC_BATCH = 32
DEC_SEQ = 1
PAST_LEN = 8192
PAGE_SIZE = 128

N_MIXERS = 3
LAYER_KINDS = tuple(i % N_MIXERS for i in range(DEPTH))
N_FOX = LAYER_KINDS.count(0)
N_SSD = LAYER_KINDS.count(1)
N_POOL_LAYERS = LAYER_KINDS.count(2)

FOX_HEADS = 16
FOX_HEAD_DIM = D_MODEL // FOX_HEADS
Q_BLOCK = 128
SSD_EXPAND = 2
SSD_D_INNER = SSD_EXPAND * D_MODEL
SSD_HEAD_DIM = 64
SSD_HEADS = SSD_D_INNER // SSD_HEAD_DIM
SSD_GROUPS = 4
SSD_STATE = 128
SSD_CONV = 4
SSD_CHUNK = 128
SSD_CONV_DIM = SSD_D_INNER + 2 * SSD_GROUPS * SSD_STATE
SSD_IN_DIM = SSD_D_INNER + SSD_CONV_DIM + SSD_HEADS
POOL_WINDOWS = (2, 4, 8, 16)
POOL_GROUPS = len(POOL_WINDOWS)
POOL_GROUP_DIM = D_MODEL // POOL_GROUPS
POOL_PAST = max(POOL_WINDOWS) - 1
D_FF = 4 * D_MODEL
EPS = 1e-6

kernel_name = "fox_ssd_pool_hybrid_step"


def rms_norm(x, g):
    xf = x.astype(jnp.float32)
    y = xf * lax.rsqrt(jnp.mean(xf * xf, axis=-1, keepdims=True) + EPS)
    return (y * g.astype(jnp.float32)).astype(x.dtype)


def sq_relu_mlp(xn, w_up, w_down):
    h = jax.nn.relu(xn @ w_up)
    return (h * h) @ w_down


def fox_project(xn, w_qkv, w_f, b_f):
    b, l, _ = xn.shape
    q, k, v = jnp.split(xn @ w_qkv, 3, axis=-1)
    shp = (b, l, FOX_HEADS, FOX_HEAD_DIM)
    logf = jax.nn.log_sigmoid((xn @ w_f + b_f).astype(jnp.float32))
    return q.reshape(shp), k.reshape(shp), v.reshape(shp), logf


def fox_attend(q, c_q, pos_q, segments):
    scale = FOX_HEAD_DIM ** -0.5
    cq = jnp.swapaxes(c_q, 1, 2)[..., :, None]
    logits = []
    for k, v, c_k, pos_k in segments:
        s = jnp.einsum('bqhd,bkhd->bhqk', q, k).astype(jnp.float32) * scale
        s = s + cq - jnp.swapaxes(c_k, 1, 2)[..., None, :]
        logits.append(jnp.where(pos_k[None, :] <= pos_q[:, None], s, -jnp.inf))
    p = jax.nn.softmax(jnp.concatenate(logits, axis=-1), axis=-1)
    out = 0
    start = 0
    for k, v, c_k, pos_k in segments:
        n = k.shape[1]
        out = out + jnp.einsum('bhqk,bkhd->bqhd', p[..., start:start + n].astype(v.dtype), v)
        start += n
    return out


def fox_prompt(xn, w_qkv, w_f, b_f, w_o):
    b, l, _ = xn.shape
    q, k, v, logf = fox_project(xn, w_qkv, w_f, b_f)
    c = jnp.cumsum(logf, axis=1)
    pos = jnp.arange(l)

    def one_block(i):
        start = i * Q_BLOCK
        q_b = lax.dynamic_slice_in_dim(q, start, Q_BLOCK, axis=1)
        c_b = lax.dynamic_slice_in_dim(c, start, Q_BLOCK, axis=1)
        p_b = start + jnp.arange(Q_BLOCK)
        return fox_attend(q_b, c_b, p_b, [(k, v, c, pos)])

    o = lax.map(one_block, jnp.arange(l // Q_BLOCK))
    o = jnp.moveaxis(o, 0, 1).reshape(b, l, D_MODEL)
    return (o @ w_o).astype(xn.dtype), k, v, logf


def fox_sample(xn, k_past, v_past, f_past, w_qkv, w_f, b_f, w_o):
    b, l, _ = xn.shape
    past = k_past.shape[1]
    q, k, v, logf = fox_project(xn, w_qkv, w_f, b_f)
    c_all = jnp.cumsum(jnp.concatenate([f_past.astype(jnp.float32), logf], axis=1), axis=1)
    c_past, c_new = c_all[:, :past], c_all[:, past:]
    pos_past = jnp.arange(past)
    pos_new = past + jnp.arange(l)
    o = fox_attend(q, c_new, pos_new,
                   [(k_past, v_past, c_past, pos_past), (k, v, c_new, pos_new)])
    return (o.reshape(b, l, D_MODEL) @ w_o).astype(xn.dtype), k, v, logf


def causal_dwconv(u, prefix, w, bias):
    l = u.shape[1]
    up = jnp.concatenate([prefix.astype(u.dtype), u], axis=1)
    y = sum(up[:, j:j + l] * w[j] for j in range(SSD_CONV)) + bias
    return y, up[:, -(SSD_CONV - 1):]


def ssd_scan(x, dt, A, B, C, h0, chunk):
    f32 = jnp.float32
    b, l, h, p = x.shape
    g, n = B.shape[2], B.shape[3]
    r = h // g
    nc = l // chunk
    xdt = (x.astype(f32) * dt[..., None]).reshape(b, nc, chunk, g, r, p)
    a_cum = jnp.cumsum((dt * A).reshape(b, nc, chunk, g, r), axis=2)
    Bc = B.astype(f32).reshape(b, nc, chunk, g, n)
    Cc = C.astype(f32).reshape(b, nc, chunk, g, n)
    seg = a_cum[:, :, :, None] - a_cum[:, :, None, :]
    causal = jnp.tril(jnp.ones((chunk, chunk), bool))[:, :, None, None]
    decay = jnp.where(causal, jnp.exp(jnp.where(causal, seg, 0.0)), 0.0)
    cb = jnp.einsum('bctgn,bcsgn->bctsg', Cc, Bc)
    y_diag = jnp.einsum('bctsg,bctsgr,bcsgrp->bctgrp', cb, decay, xdt)
    decay_end = jnp.exp(a_cum[:, :, -1:] - a_cum)
    states = jnp.einsum('bcsgn,bcsgr,bcsgrp->bcgrpn', Bc, decay_end, xdt)
    chunk_decay = jnp.exp(a_cum[:, :, -1])

    def step(hc, inp):
        st, dc = inp
        return hc * dc[..., None, None] + st, hc

    h_init = h0.astype(f32).reshape(b, g, r, p, n)
    h_last, h_in = lax.scan(step, h_init,
                            (jnp.moveaxis(states, 1, 0), jnp.moveaxis(chunk_decay, 1, 0)))
    h_in = jnp.moveaxis(h_in, 0, 1)
    y_off = jnp.einsum('bctgn,bcgrpn,bctgr->bctgrp', Cc, h_in, jnp.exp(a_cum))
    y = (y_diag + y_off).reshape(b, l, h, p)
    return y, h_last.reshape(b, h, p, n)


def ssd_mixer(xn, conv_prefix, h0, w_in, conv_w, conv_b, dt_bias, a_log, d_skip,
              norm_g, w_out, chunk):
    b, l, _ = xn.shape
    z, xbc, dt = jnp.split(xn @ w_in, [SSD_D_INNER, SSD_D_INNER + SSD_CONV_DIM], axis=-1)
    xbc_c, new_conv = causal_dwconv(xbc, conv_prefix, conv_w, conv_b)
    xbc_c = jax.nn.silu(xbc_c)
    xs, Bm, Cm = jnp.split(xbc_c, [SSD_D_INNER, SSD_D_INNER + SSD_GROUPS * SSD_STATE], axis=-1)
    xs = xs.reshape(b, l, SSD_HEADS, SSD_HEAD_DIM)
    Bm = Bm.reshape(b, l, SSD_GROUPS, SSD_STATE)
    Cm = Cm.reshape(b, l, SSD_GROUPS, SSD_STATE)
    dt = jax.nn.softplus((dt + dt_bias).astype(jnp.float32))
    A = -jnp.exp(a_log.astype(jnp.float32))
    y, h_final = ssd_scan(xs, dt, A, Bm, Cm, h0, chunk)
    y = y + xs.astype(jnp.float32) * d_skip.astype(jnp.float32)[:, None]
    yg = (y.reshape(b, l, SSD_D_INNER) * jax.nn.silu(z.astype(jnp.float32)))
    yg = yg.reshape(b, l, SSD_GROUPS, SSD_D_INNER // SSD_GROUPS)
    yg = yg * lax.rsqrt(jnp.mean(yg * yg, axis=-1, keepdims=True) + EPS)
    yg = (yg.reshape(b, l, SSD_D_INNER) * norm_g.astype(jnp.float32)).astype(xn.dtype)
    return (yg @ w_out).astype(xn.dtype), h_final, new_conv


def pool_mixer(xn, prefix, pos, w_grp, scale):
    b, l, _ = xn.shape
    f32 = jnp.float32
    xp = jnp.concatenate([prefix.astype(f32), xn.astype(f32)], axis=1)
    cs = jnp.cumsum(xp, axis=1)
    cs = jnp.concatenate([jnp.zeros_like(cs[:, :1]), cs], axis=1)
    outs = []
    for gi, w in enumerate(POOL_WINDOWS):
        sl = slice(gi * POOL_GROUP_DIM, (gi + 1) * POOL_GROUP_DIM)
        hi = cs[:, POOL_PAST + 1:POOL_PAST + 1 + l, sl]
        lo = cs[:, POOL_PAST + 1 - w:POOL_PAST + 1 - w + l, sl]
        cnt = jnp.minimum(pos + 1, w).astype(f32)[None, :, None]
        outs.append((hi - lo) / cnt - xp[:, POOL_PAST:, sl])
    pooled = jnp.stack(outs, axis=2)
    mixed = jnp.einsum('blgc,gcd->blgd', pooled, w_grp.astype(f32)).reshape(b, l, D_MODEL)
    new_prefix = xp[:, -POOL_PAST:]
    return (mixed * scale.astype(f32)).astype(xn.dtype), new_prefix.astype(xn.dtype)


def setup_inputs(seed: int = 0) -> dict:
    key = jax.random.key(seed)
    ks = iter(jax.random.split(key, 48))
    f32 = jnp.float32

    def nrm(shape, s):
        return jax.random.normal(next(ks), shape, f32) * s

    n_pages = PAST_LEN // PAGE_SIZE
    n_used = DEC_BATCH * n_pages
    n_pool_pages = n_used + n_used // 4
    page_table = jax.random.permutation(next(ks), n_pool_pages)[:n_used]
    page_table = page_table.reshape(DEC_BATCH, n_pages).astype(jnp.int32)

    dt0 = jnp.exp(jax.random.uniform(next(ks), (N_SSD, SSD_HEADS), f32,
                                     np.log(1e-3), np.log(1e-1)))
    return {
        "x_prompt": nrm((BATCH, SEQ, D_MODEL), 1.0),
        "x_sample": nrm((DEC_BATCH, DEC_SEQ, D_MODEL), 1.0),
        "cache_k": nrm((N_FOX, n_pool_pages, PAGE_SIZE, FOX_HEADS, FOX_HEAD_DIM), 1.0),
        "cache_v": nrm((N_FOX, n_pool_pages, PAGE_SIZE, FOX_HEADS, FOX_HEAD_DIM), 1.0),
        "cache_logf": jax.nn.log_sigmoid(3.0 + nrm((N_FOX, n_pool_pages, PAGE_SIZE, FOX_HEADS), 1.5)),
        "page_table": page_table,
        "state_ssm": nrm((N_SSD, DEC_BATCH, SSD_HEADS, SSD_HEAD_DIM, SSD_STATE), 0.5),
        "state_conv": nrm((N_SSD, DEC_BATCH, SSD_CONV - 1, SSD_CONV_DIM), 1.0),
        "state_pool": nrm((N_POOL_LAYERS, DEC_BATCH, POOL_PAST, D_MODEL), 1.0),
        "norm_mix": 1.0 + nrm((DEPTH, D_MODEL), 0.1),
        "norm_mlp": 1.0 + nrm((DEPTH, D_MODEL), 0.1),
        "final_norm": 1.0 + nrm((D_MODEL,), 0.1),
        "fox_w_qkv": nrm((N_FOX, D_MODEL, 3 * D_MODEL), D_MODEL ** -0.5),
        "fox_w_f": nrm((N_FOX, D_MODEL, FOX_HEADS), D_MODEL ** -0.5),
        "fox_b_f": jax.random.uniform(next(ks), (N_FOX, FOX_HEADS), f32, 1.0, 6.0),
        "fox_w_o": nrm((N_FOX, D_MODEL, D_MODEL), D_MODEL ** -0.5),
        "ssd_w_in": nrm((N_SSD, D_MODEL, SSD_IN_DIM), D_MODEL ** -0.5),
        "ssd_conv_w": nrm((N_SSD, SSD_CONV, SSD_CONV_DIM), SSD_CONV ** -0.5),
        "ssd_conv_b": nrm((N_SSD, SSD_CONV_DIM), 0.02),
        "ssd_dt_bias": dt0 + jnp.log(-jnp.expm1(-dt0)),
        "ssd_a_log": jnp.log(jax.random.uniform(next(ks), (N_SSD, SSD_HEADS), f32, 1.0, 16.0)),
        "ssd_d": 1.0 + nrm((N_SSD, SSD_HEADS), 0.1),
        "ssd_norm": 1.0 + nrm((N_SSD, SSD_D_INNER), 0.1),
        "ssd_w_out": nrm((N_SSD, SSD_D_INNER, D_MODEL), SSD_D_INNER ** -0.5),
        "pool_w": nrm((N_POOL_LAYERS, POOL_GROUPS, POOL_GROUP_DIM, POOL_GROUP_DIM), POOL_GROUP_DIM ** -0.5),
        "pool_scale": 1.0 + nrm((N_POOL_LAYERS, D_MODEL), 0.1),
        "mlp_w_up": nrm((DEPTH, D_MODEL, D_FF), D_MODEL ** -0.5),
        "mlp_w_down": nrm((DEPTH, D_FF, D_MODEL), D_FF ** -0.5),
    }


def reference(x_prompt, x_sample, cache_k, cache_v, cache_logf, page_table, state_ssm,
              state_conv, state_pool, norm_mix, norm_mlp, final_norm, fox_w_qkv, fox_w_f,
              fox_b_f, fox_w_o, ssd_w_in, ssd_conv_w, ssd_conv_b, ssd_dt_bias, ssd_a_log,
              ssd_d, ssd_norm, ssd_w_out, pool_w, pool_scale, mlp_w_up, mlp_w_down):
    b_p, l_p, _ = x_prompt.shape
    b_s, l_s, _ = x_sample.shape
    past = page_table.shape[1] * PAGE_SIZE
    pos_p = jnp.arange(l_p)
    pos_s = past + jnp.arange(l_s)
    hp, hs = x_prompt, x_sample
    k_p, v_p, f_p, k_s, v_s, f_s = [], [], [], [], [], []
    ssm_p, conv_p, ssm_s, conv_s = [], [], [], []
    pool_p, pool_s = [], []
    i_fox = i_ssd = i_pool = 0
    for layer in range(DEPTH):
        kind = LAYER_KINDS[layer]
        xn_p = rms_norm(hp, norm_mix[layer])
        xn_s = rms_norm(hs, norm_mix[layer])
        if kind == 0:
            j = i_fox
            o_p, kk, vv, ff = fox_prompt(xn_p, fox_w_qkv[j], fox_w_f[j], fox_b_f[j], fox_w_o[j])
            k_p.append(kk); v_p.append(vv); f_p.append(ff)
            k_past = cache_k[j, page_table].reshape(b_s, past, FOX_HEADS, FOX_HEAD_DIM)
            v_past = cache_v[j, page_table].reshape(b_s, past, FOX_HEADS, FOX_HEAD_DIM)
            f_past = cache_logf[j, page_table].reshape(b_s, past, FOX_HEADS)
            o_s, kk, vv, ff = fox_sample(xn_s, k_past, v_past, f_past, fox_w_qkv[j],
                                         fox_w_f[j], fox_b_f[j], fox_w_o[j])
            k_s.append(kk); v_s.append(vv); f_s.append(ff)
            i_fox += 1
        elif kind == 1:
            j = i_ssd
            w = (ssd_w_in[j], ssd_conv_w[j], ssd_conv_b[j], ssd_dt_bias[j], ssd_a_log[j],
                 ssd_d[j], ssd_norm[j], ssd_w_out[j])
            conv0 = jnp.zeros((b_p, SSD_CONV - 1, SSD_CONV_DIM), x_prompt.dtype)
            h0 = jnp.zeros((b_p, SSD_HEADS, SSD_HEAD_DIM, SSD_STATE), jnp.float32)
            o_p, hh, cc = ssd_mixer(xn_p, conv0, h0, *w, chunk=SSD_CHUNK)
            ssm_p.append(hh); conv_p.append(cc)
            o_s, hh, cc = ssd_mixer(xn_s, state_conv[j], state_ssm[j], *w, chunk=l_s)
            ssm_s.append(hh); conv_s.append(cc)
            i_ssd += 1
        else:
            j = i_pool
            pre0 = jnp.zeros((b_p, POOL_PAST, D_MODEL), x_prompt.dtype)
            o_p, pp = pool_mixer(xn_p, pre0, pos_p, pool_w[j], pool_scale[j])
            pool_p.append(pp)
            o_s, pp = pool_mixer(xn_s, state_pool[j], pos_s, pool_w[j], pool_scale[j])
            pool_s.append(pp)
            i_pool += 1
        hp = hp + o_p
        hs = hs + o_s
        hp = hp + sq_relu_mlp(rms_norm(hp, norm_mlp[layer]), mlp_w_up[layer], mlp_w_down[layer])
        hs = hs + sq_relu_mlp(rms_norm(hs, norm_mlp[layer]), mlp_w_up[layer], mlp_w_down[layer])
    y_prompt = rms_norm(hp, final_norm)
    y_sample = rms_norm(hs, final_norm)
    return (y_prompt, y_sample,
            jnp.stack(k_p), jnp.stack(v_p), jnp.stack(f_p),
            jnp.stack(k_s), jnp.stack(v_s), jnp.stack(f_s),
            jnp.stack(ssm_p), jnp.stack(conv_p), jnp.stack(ssm_s), jnp.stack(conv_s),
            jnp.stack(pool_p), jnp.stack(pool_s))
```

```python
import functools

import jax
import jax.numpy as jnp
from jax import lax
from jax.experimental import pallas as pl
from jax.experimental.pallas import tpu as pltpu

F32 = jnp.float32
BF16 = jnp.bfloat16
EPS = 1e-6
NEG = -1e30
LANES = 128
SSD_CHUNK = 128
POOL_WINDOWS = (2, 4, 8, 16)
VMEM_LIMIT = 48 * 1024 * 1024
HIGHEST = lax.Precision.HIGHEST
NT_DIMS = (((1,), (1,)), ((), ()))


def _params(*sem):
    return pltpu.CompilerParams(dimension_semantics=sem, vmem_limit_bytes=VMEM_LIMIT)


def _rms(x, g):
    return x * lax.rsqrt(jnp.mean(x * x, axis=-1, keepdims=True) + EPS) * g


def _softplus(x):
    return jnp.maximum(x, 0.0) + jnp.log1p(jnp.exp(-jnp.abs(x)))


def _sigmoid(x):
    return 1.0 / (1.0 + jnp.exp(-x))


def _norm_linear_kernel(x_ref, g_ref, w_ref, b_ref, *refs, act, outs):
    out_refs, xn_ref = refs[:-1], refs[-1]

    @pl.when(pl.program_id(1) == 0)
    def _():
        xn_ref[...] = _rms(x_ref[...], g_ref[...]).astype(BF16)

    y = jnp.dot(xn_ref[...], w_ref[...], preferred_element_type=F32) + b_ref[...]
    if act == "log_sigmoid":
        y = -_softplus(-y)
    elif act == "softplus":
        y = _softplus(y)
    for o_ref, (dtype, scale) in zip(out_refs, outs):
        o_ref[...] = (y * scale).astype(dtype)


def _norm_linear(x, g, w, b=None, *, act=None, outs=((F32, 1.0),), tm=512, tn=1024):
    m, k = x.shape
    n = w.shape[1]
    tm, tn = min(tm, m), min(tn, n)
    if b is None:
        b = jnp.zeros((n,), F32)
    res = pl.pallas_call(
        functools.partial(_norm_linear_kernel, act=act, outs=outs),
        grid=(m // tm, n // tn),
        in_specs=[
            pl.BlockSpec((tm, k), lambda i, j: (i, 0)),
            pl.BlockSpec((1, k), lambda i, j: (0, 0)),
            pl.BlockSpec((k, tn), lambda i, j: (0, j)),
            pl.BlockSpec((1, tn), lambda i, j: (0, j)),
        ],
        out_specs=[pl.BlockSpec((tm, tn), lambda i, j: (i, j)) for _ in outs],
        out_shape=[jax.ShapeDtypeStruct((m, n), d) for d, _ in outs],
        scratch_shapes=[pltpu.VMEM((tm, k), BF16)],
        compiler_params=_params("parallel", "arbitrary"),
        name="norm_linear",
    )(x, g.reshape(1, k), w, b.reshape(1, n))
    return res


def _linear_res_kernel(a_ref, w_ref, r_ref, o_ref):
    o_ref[...] = r_ref[...] + jnp.dot(a_ref[...], w_ref[...], preferred_element_type=F32)


def _linear_res(a, w, res, *, tm=512):
    m, k = a.shape
    n = w.shape[1]
    tm = min(tm, m)
    return pl.pallas_call(
        _linear_res_kernel,
        grid=(m // tm,),
        in_specs=[
            pl.BlockSpec((tm, k), lambda i: (i, 0)),
            pl.BlockSpec((k, n), lambda i: (0, 0)),
            pl.BlockSpec((tm, n), lambda i: (i, 0)),
        ],
        out_specs=pl.BlockSpec((tm, n), lambda i: (i, 0)),
        out_shape=jax.ShapeDtypeStruct((m, n), F32),
        compiler_params=_params("parallel"),
        name="linear_res",
    )(a, w, res)


def _mlp_kernel(x_ref, g_ref, wu_ref, wd_ref, o_ref, xn_ref, acc_ref):
    f = pl.program_id(1)

    @pl.when(f == 0)
    def _():
        xn_ref[...] = _rms(x_ref[...], g_ref[...]).astype(BF16)
        acc_ref[...] = jnp.zeros_like(acc_ref)

    h = jnp.maximum(jnp.dot(xn_ref[...], wu_ref[...], preferred_element_type=F32), 0.0)
    acc_ref[...] += jnp.dot((h * h).astype(BF16), wd_ref[...], preferred_element_type=F32)

    @pl.when(f == pl.num_programs(1) - 1)
    def _():
        o_ref[...] = x_ref[...] + acc_ref[...]


def _mlp(x, g, w_up, w_down, *, tm=1024, tf=512):
    m, d = x.shape
    ff = w_up.shape[1]
    tm = min(tm, m)
    return pl.pallas_call(
        _mlp_kernel,
        grid=(m // tm, ff // tf),
        in_specs=[
            pl.BlockSpec((tm, d), lambda i, f: (i, 0)),
            pl.BlockSpec((1, d), lambda i, f: (0, 0)),
            pl.BlockSpec((d, tf), lambda i, f: (0, f)),
            pl.BlockSpec((tf, d), lambda i, f: (f, 0)),
        ],
        out_specs=pl.BlockSpec((tm, d), lambda i, f: (i, 0)),
        out_shape=jax.ShapeDtypeStruct((m, d), F32),
        scratch_shapes=[pltpu.VMEM((tm, d), BF16), pltpu.VMEM((tm, d), F32)],
        compiler_params=_params("parallel", "arbitrary"),
        name="mlp",
    )(x, g.reshape(1, d), w_up, w_down)


def _norm_kernel(x_ref, g_ref, o_ref):
    o_ref[...] = _rms(x_ref[...], g_ref[...])


def _norm(x, g, *, tm=1024):
    m, d = x.shape
    tm = min(tm, m)
    return pl.pallas_call(
        _norm_kernel,
        grid=(m // tm,),
        in_specs=[pl.BlockSpec((tm, d), lambda i: (i, 0)), pl.BlockSpec((1, d), lambda i: (0, 0))],
        out_specs=pl.BlockSpec((tm, d), lambda i: (i, 0)),
        out_shape=jax.ShapeDtypeStruct((m, d), F32),
        compiler_params=_params("parallel"),
        name="final_norm",
    )(x, g.reshape(1, d))


def _cumsum_kernel(x_ref, o_ref):
    rows, length = x_ref.shape
    r = lax.broadcasted_iota(jnp.int32, (LANES, LANES), 0)
    c = lax.broadcasted_iota(jnp.int32, (LANES, LANES), 1)
    upper = (r <= c).astype(F32)

    carry = jnp.zeros((rows, 1), F32)
    for j in range(length // LANES):
        sl = slice(j * LANES, (j + 1) * LANES)
        cs = jnp.dot(x_ref[:, sl], upper, precision=HIGHEST, preferred_element_type=F32) + carry
        o_ref[:, sl] = cs
        carry = cs[:, LANES - 1:LANES]


def _cumsum_lanes(x):
    b, rows, length = x.shape
    return pl.pallas_call(
        _cumsum_kernel,
        grid=(b,),
        in_specs=[pl.BlockSpec((None, rows, length), lambda i: (i, 0, 0))],
        out_specs=pl.BlockSpec((None, rows, length), lambda i: (i, 0, 0)),
        out_shape=jax.ShapeDtypeStruct(x.shape, F32),
        compiler_params=_params("parallel"),
        name="cumsum_lanes",
    )(x)


def _fox_attn_kernel(q_ref, k_ref, v_ref, ct_ref, o_ref, m_ref, l_ref, acc_ref, *, tq, hd):
    pair, i = pl.program_id(1), pl.program_id(2)
    q2 = q_ref[...]
    lo = lax.broadcasted_iota(jnp.int32, (tq, 2 * hd), 1) < hd
    zero = jnp.zeros_like(q2)
    qq = jnp.concatenate([jnp.where(lo, q2, zero), jnp.where(lo, zero, q2)], axis=0)
    m_ref[...] = jnp.full_like(m_ref, NEG)
    l_ref[...] = jnp.zeros_like(l_ref)
    acc_ref[...] = jnp.zeros_like(acc_ref)
    h0 = 2 * pair

    def step(j, masked):
        sl = pl.ds(pl.multiple_of(j * tq, tq), tq)
        s = lax.dot_general(qq, k_ref[sl, :], NT_DIMS, preferred_element_type=F32)
        s = jnp.concatenate([s[:tq] - ct_ref[pl.ds(h0, 1), sl],
                             s[tq:] - ct_ref[pl.ds(h0 + 1, 1), sl]], axis=0)
        if masked:
            keep = (lax.broadcasted_iota(jnp.int32, (tq, tq), 1)
                    <= lax.broadcasted_iota(jnp.int32, (tq, tq), 0))
            s = jnp.where(jnp.concatenate([keep, keep], axis=0), s, NEG)
        m_prev = m_ref[...]
        m_new = jnp.maximum(m_prev, jnp.max(s, axis=1, keepdims=True))
        alpha = jnp.exp(m_prev - m_new)
        p = jnp.exp(s - m_new[:, :1])
        l_ref[...] = alpha * l_ref[...] + jnp.sum(p, axis=1, keepdims=True)
        acc_ref[...] = alpha * acc_ref[...] + jnp.dot(p.astype(BF16), v_ref[sl, :],
                                                      preferred_element_type=F32)
        m_ref[...] = m_new

    def full_step(j, carry):
        step(j, False)
        return carry

    lax.fori_loop(0, i, full_step, 0)
    step(i, True)
    out = acc_ref[...] / l_ref[...]
    o_ref[...] = jnp.where(lo, out[:tq], out[tq:]).astype(o_ref.dtype)


def _fox_attn(q, k, v, ct, *, batch, tq=512):
    t, d = q.shape
    heads = ct.shape[1]
    hd = d // heads
    seq = t // batch
    tq = min(tq, seq)
    nq = seq // tq
    k3, v3 = k.reshape(batch, seq, d), v.reshape(batch, seq, d)
    return pl.pallas_call(
        functools.partial(_fox_attn_kernel, tq=tq, hd=hd),
        grid=(batch, heads // 2, nq),
        in_specs=[
            pl.BlockSpec((tq, 2 * hd), lambda b, p, i: (b * nq + i, p)),
            pl.BlockSpec((None, seq, 2 * hd), lambda b, p, i: (b, 0, p)),
            pl.BlockSpec((None, seq, 2 * hd), lambda b, p, i: (b, 0, p)),
            pl.BlockSpec((None, heads, seq), lambda b, p, i: (b, 0, 0)),
        ],
        out_specs=pl.BlockSpec((tq, 2 * hd), lambda b, p, i: (b * nq + i, p)),
        out_shape=jax.ShapeDtypeStruct((t, d), BF16),
        scratch_shapes=[pltpu.VMEM((2 * tq, LANES), F32), pltpu.VMEM((2 * tq, LANES), F32),
                        pltpu.VMEM((2 * tq, 2 * hd), F32)],
        compiler_params=_params("parallel", "parallel", "arbitrary"),
        name="fox_attn",
    )(q, k3, v3, ct)


def _strided_prefix(x, heads):
    n = x.shape[1]
    lane = lax.broadcasted_iota(jnp.int32, x.shape, 1)
    c, tot = x, x
    sh = heads
    while sh < n:
        c = c + jnp.where(lane >= sh, pltpu.roll(c, sh, axis=1), 0.0)
        tot = tot + pltpu.roll(tot, sh, axis=1)
        sh *= 2
    return c, tot


def _fox_decode_kernel(pt_ref, q_ref, kn_ref, vn_ref, fn_ref, *refs, pages, heads, hd):
    del pt_ref
    k_refs, v_refs, f_refs = refs[:pages], refs[pages:2 * pages], refs[2 * pages:3 * pages]
    o_ref, m_ref, l_ref, acc_ref, cs_ref = refs[3 * pages:]
    step = pl.program_id(1)
    psz = k_refs[0].shape[0]
    width = psz * heads

    @pl.when(step == 0)
    def _():
        m_ref[...] = jnp.full_like(m_ref, NEG)
        l_ref[...] = jnp.zeros_like(l_ref)
        acc_ref[...] = jnp.zeros_like(acc_ref)
        cs_ref[...] = jnp.zeros_like(cs_ref)

    q = q_ref[...]
    row = lax.broadcasted_iota(jnp.int32, (heads, width), 0)
    col = lax.broadcasted_iota(jnp.int32, (heads, width), 1)
    own = (col % heads) == row
    for r in range(pages):
        k2 = k_refs[r][...].reshape(width, hd).astype(BF16)
        v2 = v_refs[r][...].reshape(width, hd).astype(BF16)
        f8 = jnp.broadcast_to(f_refs[r][...], (8, width))
        c8, tot8 = _strided_prefix(f8, heads)
        carry = cs_ref[...]
        s = lax.dot_general(q, k2, NT_DIMS, preferred_element_type=F32)
        s = jnp.where(own, s - (carry[:1] + c8[:1]), NEG)
        cs_ref[...] = carry + tot8
        m_prev = m_ref[...]
        m_new = jnp.maximum(m_prev, jnp.max(s, axis=1, keepdims=True))
        alpha = jnp.exp(m_prev - m_new)
        p = jnp.exp(s - m_new[:, :1])
        l_ref[...] = alpha * l_ref[...] + jnp.sum(p, axis=1, keepdims=True)
        acc_ref[...] = alpha[:, :hd] * acc_ref[...] + jnp.dot(p.astype(BF16), v2,
                                                              preferred_element_type=F32)
        m_ref[...] = m_new

    @pl.when(step == pl.num_programs(1) - 1)
    def _():
        diag = (lax.broadcasted_iota(jnp.int32, (heads, heads), 0)
                == lax.broadcasted_iota(jnp.int32, (heads, heads), 1))
        c_new = cs_ref[:1, :heads] + fn_ref[...]
        c_col = jnp.sum(jnp.where(diag, jnp.broadcast_to(c_new, (heads, heads)), 0.0),
                        axis=1, keepdims=True)
        s_new = jnp.sum(q.astype(F32) * kn_ref[...], axis=1, keepdims=True) - c_col
        m_prev = m_ref[...]
        m_new = jnp.maximum(m_prev, s_new)
        alpha = jnp.exp(m_prev - m_new)
        p_new = jnp.exp(s_new - m_new[:, :1])
        l_fin = alpha[:, :1] * l_ref[:, :1] + p_new
        acc = alpha[:, :hd] * acc_ref[...] + p_new * vn_ref[...]
        o_ref[...] = acc / l_fin


def _fox_decode(q, k_new, v_new, f_new, cache_k, cache_v, logf_flat, page_table, layer, *, pages=4):
    rows, heads, hd = q.shape
    psz = cache_k.shape[2]
    n_pages = page_table.shape[1]
    pages = min(pages, n_pages)
    width = psz * heads

    def page_map(r):
        return lambda b, s, pt: (layer, pt[b * n_pages + s * pages + r], 0, 0, 0)

    def flat_map(r):
        return lambda b, s, pt: (layer, pt[b * n_pages + s * pages + r], 0, 0)

    row_spec = pl.BlockSpec((None, heads, hd), lambda b, s, pt: (b, 0, 0))
    in_specs = [row_spec, row_spec, row_spec,
                pl.BlockSpec((None, 1, heads), lambda b, s, pt: (b, 0, 0))]
    in_specs += [pl.BlockSpec((None, None, psz, heads, hd), page_map(r)) for r in range(pages)]
    in_specs += [pl.BlockSpec((None, None, psz, heads, hd), page_map(r)) for r in range(pages)]
    in_specs += [pl.BlockSpec((None, None, 1, width), flat_map(r)) for r in range(pages)]
    return pl.pallas_call(
        functools.partial(_fox_decode_kernel, pages=pages, heads=heads, hd=hd),
        grid_spec=pltpu.PrefetchScalarGridSpec(
            num_scalar_prefetch=1,
            grid=(rows, n_pages // pages),
            in_specs=in_specs,
            out_specs=pl.BlockSpec((None, heads, hd), lambda b, s, pt: (b, 0, 0)),
            scratch_shapes=[pltpu.VMEM((heads, LANES), F32), pltpu.VMEM((heads, LANES), F32),
                            pltpu.VMEM((heads, hd), F32), pltpu.VMEM((8, width), F32)],
        ),
        out_shape=jax.ShapeDtypeStruct((rows, heads, hd), F32),
        compiler_params=_params("parallel", "arbitrary"),
        name="fox_decode",
    )(page_table.reshape(-1), q, k_new, v_new, f_new,
      *([cache_k] * pages), *([cache_v] * pages), *([logf_flat] * pages))


def _conv_kernel(u_ref, w_ref, b_ref, xs_ref, bm_ref, cm_ref, ext_ref, *, taps):
    tl = u_ref.shape[0]
    pad = 8

    @pl.when(pl.program_id(1) == 0)
    def _():
        ext_ref[0:pad, :] = jnp.zeros((pad, ext_ref.shape[1]), F32)

    @pl.when(pl.program_id(1) > 0)
    def _():
        ext_ref[0:pad, :] = ext_ref[tl:tl + pad, :]

    u = u_ref[...]
    ext_ref[pad:pad + tl, :] = u
    y = w_ref[0:1, :] * ext_ref[pl.ds(pad - (taps - 1), tl), :]
    for j in range(1, taps - 1):
        y = y + w_ref[j:j + 1, :] * ext_ref[pl.ds(pad - (taps - 1 - j), tl), :]
    y = y + w_ref[taps - 1:taps, :] * u + b_ref[...]
    y = y * _sigmoid(y)
    di, gn = xs_ref.shape[1], bm_ref.shape[1]
    xs_ref[...] = y[:, :di]
    bm_ref[...] = y[:, di:di + gn]
    cm_ref[...] = y[:, di + gn:]


def _ssd_conv(xbc, w, b, *, batch, di, gn, tl=256):
    t, cdim = xbc.shape
    seq = t // batch
    tl = min(tl, seq)
    nl = seq // tl
    taps = w.shape[0]
    return pl.pallas_call(
        functools.partial(_conv_kernel, taps=taps),
        grid=(batch, nl),
        in_specs=[
            pl.BlockSpec((tl, cdim), lambda bb, l: (bb * nl + l, 0)),
            pl.BlockSpec((taps, cdim), lambda bb, l: (0, 0)),
            pl.BlockSpec((1, cdim), lambda bb, l: (0, 0)),
        ],
        out_specs=[
            pl.BlockSpec((tl, di), lambda bb, l: (bb * nl + l, 0)),
            pl.BlockSpec((tl, gn), lambda bb, l: (bb * nl + l, 0)),
            pl.BlockSpec((tl, gn), lambda bb, l: (bb * nl + l, 0)),
        ],
        out_shape=[jax.ShapeDtypeStruct((t, di), F32), jax.ShapeDtypeStruct((t, gn), F32),
                   jax.ShapeDtypeStruct((t, gn), F32)],
        scratch_shapes=[pltpu.VMEM((tl + 8, cdim), F32)],
        compiler_params=_params("parallel", "arbitrary"),
        name="ssd_conv",
    )(xbc, w, b.reshape(1, cdim))


def _conv_step_kernel(u_ref, st_ref, w_ref, b_ref, o_ref):
    taps = w_ref.shape[0]
    y = w_ref[taps - 1:taps, :] * u_ref[...] + b_ref[...]
    for j in range(taps - 1):
        y = y + w_ref[j:j + 1, :] * st_ref[j]
    o_ref[...] = y * _sigmoid(y)


def _ssd_conv_step(u, state_t, w, b):
    rows, cdim = u.shape
    return pl.pallas_call(
        _conv_step_kernel,
        out_shape=jax.ShapeDtypeStruct((rows, cdim), F32),
        compiler_params=pltpu.CompilerParams(vmem_limit_bytes=VMEM_LIMIT),
        name="ssd_conv_step",
    )(u, state_t, w, b.reshape(1, cdim))


def _ssd_scan_kernel(xs_ref, dt_ref, bm_ref, cm_ref, a_ref, dsk_ref, y_ref, st_ref, ht_ref, *, hpg, hd):
    grp, c = pl.program_id(1), pl.program_id(2)
    q = xs_ref.shape[0]
    width = hpg * hd

    @pl.when(c == 0)
    def _():
        ht_ref[...] = jnp.zeros_like(ht_ref)

    ri = lax.broadcasted_iota(jnp.int32, (q, q), 0)
    ci = lax.broadcasted_iota(jnp.int32, (q, q), 1)
    causal = ci <= ri
    tri = causal.astype(F32)
    dt = dt_ref[...]
    acum = jnp.dot(tri, dt * a_ref[...], precision=HIGHEST, preferred_element_type=F32)
    er = lax.broadcasted_iota(jnp.int32, (LANES, width), 0)
    ec = lax.broadcasted_iota(jnp.int32, (LANES, width), 1)
    expand = (er == grp * hpg + ec // hd).astype(F32)
    sr = lax.broadcasted_iota(jnp.int32, (LANES, LANES), 0)
    sc = lax.broadcasted_iota(jnp.int32, (LANES, LANES), 1)
    select = (sr == grp * hpg + sc).astype(F32)
    acum_x = jnp.dot(acum, expand, precision=HIGHEST, preferred_element_type=F32)
    dt_x = jnp.dot(dt, expand, precision=HIGHEST, preferred_element_type=F32)
    acum_g = jnp.dot(acum, select, precision=HIGHEST, preferred_element_type=F32)
    acum_gt = acum_g.T

    xs = xs_ref[...]
    xdt = xs * dt_x
    xdt_b = xdt.astype(BF16)
    bm = bm_ref[...].astype(BF16)
    cm = cm_ref[...].astype(BF16)
    cb = lax.dot_general(cm, bm, NT_DIMS, preferred_element_type=F32)
    lo = lax.broadcasted_iota(jnp.int32, (q, 2 * hd), 1) < hd
    parts = []
    for pr in range(hpg // 2):
        xp = xdt_b[:, pr * 2 * hd:(pr + 1) * 2 * hd]
        ys = []
        for r in (2 * pr, 2 * pr + 1):
            seg = acum_g[:, r:r + 1] - acum_gt[r:r + 1, :]
            mat = (cb * jnp.exp(jnp.where(causal, seg, NEG))).astype(BF16)
            ys.append(jnp.dot(mat, xp, preferred_element_type=F32))
        parts.append(jnp.where(lo, ys[0], ys[1]))
    y_diag = jnp.concatenate(parts, axis=1)

    ht = ht_ref[...]
    y_off = jnp.dot(cm, ht.astype(BF16), preferred_element_type=F32) * jnp.exp(acum_x)
    y_ref[...] = y_diag + y_off + xs * dsk_ref[...]

    a_last = acum_x[q - 1:q, :]
    xw = (xdt * jnp.exp(a_last - acum_x)).astype(BF16)
    st = jnp.dot(bm_ref[...].T.astype(BF16), xw, preferred_element_type=F32)
    ht_new = ht * jnp.exp(a_last) + st
    ht_ref[...] = ht_new

    @pl.when(c == pl.num_programs(2) - 1)
    def _():
        st_ref[...] = ht_new.T


def _ssd_scan(xs, dt, bm, cm, a_neg, d_skip, *, batch, groups, hd, chunk=SSD_CHUNK):
    t, di = xs.shape
    n = bm.shape[1] // groups
    seq = t // batch
    nc = seq // chunk
    width = di // groups
    hpg = width // hd
    heads = di // hd
    a_row = jnp.zeros((1, LANES), F32).at[0, :heads].set(a_neg)
    dsk = jnp.repeat(d_skip, hd).reshape(1, di)
    return pl.pallas_call(
        functools.partial(_ssd_scan_kernel, hpg=hpg, hd=hd),
        grid=(batch, groups, nc),
        in_specs=[
            pl.BlockSpec((chunk, width), lambda b, g, c: (b * nc + c, g)),
            pl.BlockSpec((chunk, LANES), lambda b, g, c: (b * nc + c, 0)),
            pl.BlockSpec((chunk, n), lambda b, g, c: (b * nc + c, g)),
            pl.BlockSpec((chunk, n), lambda b, g, c: (b * nc + c, g)),
            pl.BlockSpec((1, LANES), lambda b, g, c: (0, 0)),
            pl.BlockSpec((1, width), lambda b, g, c: (0, g)),
        ],
        out_specs=[
            pl.BlockSpec((chunk, width), lambda b, g, c: (b * nc + c, g)),
            pl.BlockSpec((None, width, n), lambda b, g, c: (b, g, 0)),
        ],
        out_shape=[jax.ShapeDtypeStruct((t, di), F32), jax.ShapeDtypeStruct((batch, di, n), F32)],
        scratch_shapes=[pltpu.VMEM((n, width), F32)],
        compiler_params=_params("parallel", "parallel", "arbitrary"),
        name="ssd_scan",
    )(xs, dt, bm, cm, a_row, dsk)


def _ssd_step_kernel(h_ref, xs_ref, dt_ref, a_ref, dsk_ref, bm_ref, cm_ref, y_ref, ho_ref, *, groups):
    rows = h_ref.shape[0]
    per = rows // groups
    xc, dtc = xs_ref[...], dt_ref[...]
    decay = jnp.exp(dtc * a_ref[...])
    xdt = xc * dtc
    for g in range(groups):
        sl = slice(g * per, (g + 1) * per)
        hn = h_ref[sl, :] * decay[sl] + xdt[sl] * bm_ref[g:g + 1, :]
        ho_ref[sl, :] = hn
        y_ref[sl, :] = (jnp.sum(hn * cm_ref[g:g + 1, :], axis=1, keepdims=True)
                        + xc[sl] * dsk_ref[sl, :])


def _ssd_step(h0, xs, dt, a_neg, d_skip, bm, cm, *, groups, hd):
    rows, di, n = h0.shape
    col = lambda v: v.reshape(rows, di, 1)
    a_col = jnp.repeat(a_neg, hd).reshape(di, 1)
    dsk = jnp.repeat(d_skip, hd).reshape(di, 1)
    row3 = lambda last: pl.BlockSpec((None, di, last), lambda b: (b, 0, 0))
    grp3 = pl.BlockSpec((None, groups, n), lambda b: (b, 0, 0))
    const = pl.BlockSpec((di, 1), lambda b: (0, 0))
    y, h_new = pl.pallas_call(
        functools.partial(_ssd_step_kernel, groups=groups),
        grid=(rows,),
        in_specs=[row3(n), row3(1), row3(1), const, const, grp3, grp3],
        out_specs=[row3(1), row3(n)],
        out_shape=[jax.ShapeDtypeStruct((rows, di, 1), F32), jax.ShapeDtypeStruct((rows, di, n), F32)],
        compiler_params=_params("parallel"),
        name="ssd_step",
    )(h0, col(xs), col(jnp.repeat(dt, hd, axis=1)), a_col, dsk,
      bm.reshape(rows, groups, n), cm.reshape(rows, groups, n))
    return y.reshape(rows, di), h_new


def _gated_out_kernel(y_ref, z_ref, ng_ref, w_ref, r_ref, o_ref, *, groups):
    z = z_ref[...]
    yg = y_ref[...] * (z * _sigmoid(z))
    per = yg.shape[1] // groups
    parts = []
    for g in range(groups):
        blk = yg[:, g * per:(g + 1) * per]
        parts.append(blk * lax.rsqrt(jnp.mean(blk * blk, axis=-1, keepdims=True) + EPS))
    yn = (jnp.concatenate(parts, axis=1) * ng_ref[...]).astype(BF16)
    o_ref[...] = r_ref[...] + jnp.dot(yn, w_ref[...], preferred_element_type=F32)


def _gated_out(y, z, ng, w, res, *, groups, tm=256):
    m, di = y.shape
    n = w.shape[1]
    tm = min(tm, m)
    return pl.pallas_call(
        functools.partial(_gated_out_kernel, groups=groups),
        grid=(m // tm,),
        in_specs=[
            pl.BlockSpec((tm, di), lambda i: (i, 0)),
            pl.BlockSpec((tm, di), lambda i: (i, 0)),
            pl.BlockSpec((1, di), lambda i: (0, 0)),
            pl.BlockSpec((di, n), lambda i: (0, 0)),
            pl.BlockSpec((tm, n), lambda i: (i, 0)),
        ],
        out_specs=pl.BlockSpec((tm, n), lambda i: (i, 0)),
        out_shape=jax.ShapeDtypeStruct((m, n), F32),
        compiler_params=_params("parallel"),
        name="ssd_gated_out",
    )(y, z, ng.reshape(1, di), w, res)


def _pool_kernel(x_ref, g_ref, w_ref, sc_ref, o_ref, pre_ref, ext_ref):
    tl, d = x_ref.shape
    pad = 16
    l = pl.program_id(1)

    @pl.when(l == 0)
    def _():
        ext_ref[0:pad, :] = jnp.zeros((pad, d), F32)

    @pl.when(l > 0)
    def _():
        ext_ref[0:pad, :] = ext_ref[tl:tl + pad, :]

    x = x_ref[...]
    xn = _rms(x, g_ref[...])
    ext_ref[pad:pad + tl, :] = xn
    pos = l * tl + lax.broadcasted_iota(jnp.int32, (tl, 1), 0)
    gd = d // len(POOL_WINDOWS)
    for gi, win in enumerate(POOL_WINDOWS):
        cols = slice(gi * gd, (gi + 1) * gd)
        tot = xn[:, cols]
        for j in range(1, win):
            tot = tot + ext_ref[pl.ds(pad - j, tl), cols]
        cnt = jnp.minimum(pos + 1, win).astype(F32)
        pooled = tot / cnt - xn[:, cols]
        mixed = jnp.dot(pooled.astype(BF16), w_ref[gi], preferred_element_type=F32)
        o_ref[:, cols] = x[:, cols] + mixed * sc_ref[:, cols]

    @pl.when(l == pl.num_programs(1) - 1)
    def _():
        pre_ref[...] = xn[tl - pad:, :]


def _pool(x, g, w_grp, scale, *, batch, tl=512):
    t, d = x.shape
    seq = t // batch
    tl = min(tl, seq)
    nl = seq // tl
    ng, gd = w_grp.shape[0], w_grp.shape[1]
    return pl.pallas_call(
        _pool_kernel,
        grid=(batch, nl),
        in_specs=[
            pl.BlockSpec((tl, d), lambda b, l: (b * nl + l, 0)),
            pl.BlockSpec((1, d), lambda b, l: (0, 0)),
            pl.BlockSpec((ng, gd, gd), lambda b, l: (0, 0, 0)),
            pl.BlockSpec((1, d), lambda b, l: (0, 0)),
        ],
        out_specs=[
            pl.BlockSpec((tl, d), lambda b, l: (b * nl + l, 0)),
            pl.BlockSpec((None, 16, d), lambda b, l: (b, 0, 0)),
        ],
        out_shape=[jax.ShapeDtypeStruct((t, d), F32), jax.ShapeDtypeStruct((batch, 16, d), F32)],
        scratch_shapes=[pltpu.VMEM((tl + 16, d), F32)],
        compiler_params=_params("parallel", "arbitrary"),
        name="pool",
    )(x, g.reshape(1, d), w_grp, scale.reshape(1, d))


def _pool_step_kernel(x_ref, st_ref, g_ref, w_ref, sc_ref, o_ref, xn_ref, *, pos):
    x = x_ref[...]
    xn = _rms(x, g_ref[...])
    xn_ref[...] = xn
    past = st_ref.shape[0]
    gd = x.shape[1] // len(POOL_WINDOWS)
    for gi, win in enumerate(POOL_WINDOWS):
        cols = slice(gi * gd, (gi + 1) * gd)
        tot = xn[:, cols]
        for j in range(1, win):
            tot = tot + st_ref[past - j][:, cols]
        pooled = tot / float(min(pos + 1, win)) - xn[:, cols]
        mixed = jnp.dot(pooled.astype(BF16), w_ref[gi], preferred_element_type=F32)
        o_ref[:, cols] = x[:, cols] + mixed * sc_ref[:, cols]


def _pool_step(x, state_t, g, w_grp, scale, *, pos):
    rows, d = x.shape
    return pl.pallas_call(
        functools.partial(_pool_step_kernel, pos=pos),
        out_shape=[jax.ShapeDtypeStruct((rows, d), F32), jax.ShapeDtypeStruct((rows, d), F32)],
        compiler_params=pltpu.CompilerParams(vmem_limit_bytes=VMEM_LIMIT),
        name="pool_step",
    )(x, state_t, g.reshape(1, d), w_grp, scale.reshape(1, d))


def _fox_project(h, g, w_qkv, w_f, b_f, heads):
    d = h.shape[1]
    hd = d // heads
    wq, wk, wv = (w_qkv[:, i * d:(i + 1) * d].astype(BF16) for i in range(3))
    (q,) = _norm_linear(h, g, wq, outs=((BF16, hd ** -0.5),))
    k, k16 = _norm_linear(h, g, wk, outs=((F32, 1.0), (BF16, 1.0)))
    v, v16 = _norm_linear(h, g, wv, outs=((F32, 1.0), (BF16, 1.0)))
    wf = jnp.zeros((d, LANES), BF16).at[:, :heads].set(w_f.astype(BF16))
    bf = jnp.zeros((LANES,), F32).at[:heads].set(b_f)
    (logf,) = _norm_linear(h, g, wf, bf, act="log_sigmoid")
    return q, k, v, k16, v16, logf[:, :heads]


def kernel(x_prompt, x_sample, cache_k, cache_v, cache_logf, page_table, state_ssm, state_conv, state_pool, norm_mix, norm_mlp, final_norm, fox_w_qkv, fox_w_f, fox_b_f, fox_w_o, ssd_w_in, ssd_conv_w, ssd_conv_b, ssd_dt_bias, ssd_a_log, ssd_d, ssd_norm, ssd_w_out, pool_w, pool_scale, mlp_w_up, mlp_w_down):
    bp, lp, d = x_prompt.shape
    bs, ls, _ = x_sample.shape
    assert ls == 1, "the sample group decodes one token per row"
    depth = norm_mix.shape[0]
    heads = fox_w_f.shape[-1]
    hd = d // heads
    n_layers, n_pool, psz = cache_k.shape[:3]
    past = page_table.shape[1] * psz
    ssd_heads = ssd_a_log.shape[1]
    di = ssd_norm.shape[1]
    cdim = ssd_conv_b.shape[1]
    n_state = state_ssm.shape[-1]
    ssd_hd = state_ssm.shape[-2]
    groups = (cdim - di) // (2 * n_state)
    gn = groups * n_state
    pool_past = state_pool.shape[2]

    hp = x_prompt.reshape(bp * lp, d)
    hs = x_sample.reshape(bs, d)
    logf_flat = cache_logf.reshape(n_layers, n_pool, 1, psz * heads)
    outs = {name: [] for name in ("kp", "vp", "fp", "ks", "vs", "fs", "ssm_p", "conv_p", "ssm_s",
                                  "conv_s", "pool_p", "pool_s")}
    i_fox = i_ssd = i_pool = 0
    for layer in range(depth):
        kind = layer % 3
        g = norm_mix[layer]
        if kind == 0:
            j = i_fox
            w_o = fox_w_o[j].astype(BF16)
            q, k, v, k16, v16, logf = _fox_project(hp, g, fox_w_qkv[j], fox_w_f[j], fox_b_f[j], heads)
            ct = _cumsum_lanes(jnp.swapaxes(logf.reshape(bp, lp, heads), 1, 2))
            ctx = _fox_attn(q, k16, v16, ct, batch=bp)
            hp = _linear_res(ctx, w_o, hp)
            outs["kp"].append(k.reshape(bp, lp, heads, hd))
            outs["vp"].append(v.reshape(bp, lp, heads, hd))
            outs["fp"].append(logf.reshape(bp, lp, heads))

            q, k, v, _, _, logf = _fox_project(hs, g, fox_w_qkv[j], fox_w_f[j], fox_b_f[j], heads)
            ctx = _fox_decode(q.reshape(bs, heads, hd), k.reshape(bs, heads, hd),
                              v.reshape(bs, heads, hd), logf.reshape(bs, 1, heads),
                              cache_k, cache_v, logf_flat, page_table, j)
            hs = _linear_res(ctx.reshape(bs, d).astype(BF16), w_o, hs)
            outs["ks"].append(k.reshape(bs, 1, heads, hd))
            outs["vs"].append(v.reshape(bs, 1, heads, hd))
            outs["fs"].append(logf.reshape(bs, 1, heads))
            i_fox += 1
        elif kind == 1:
            j = i_ssd
            w_in = ssd_w_in[j]
            w_z = w_in[:, :di].astype(BF16)
            w_xbc = w_in[:, di:di + cdim].astype(BF16)
            w_dt = jnp.zeros((d, LANES), BF16).at[:, :ssd_heads].set(w_in[:, di + cdim:].astype(BF16))
            b_dt = jnp.zeros((LANES,), F32).at[:ssd_heads].set(ssd_dt_bias[j])
            a_neg = -jnp.exp(ssd_a_log[j])
            w_out = ssd_w_out[j].astype(BF16)

            def in_proj(h):
                (z,) = _norm_linear(h, g, w_z)
                (xbc,) = _norm_linear(h, g, w_xbc)
                (dt,) = _norm_linear(h, g, w_dt, b_dt, act="softplus")
                return z, xbc, dt

            z, xbc, dt = in_proj(hp)
            xs, bm, cm = _ssd_conv(xbc, ssd_conv_w[j], ssd_conv_b[j], batch=bp, di=di, gn=gn)
            y, h_fin = _ssd_scan(xs, dt, bm, cm, a_neg, ssd_d[j], batch=bp, groups=groups, hd=ssd_hd)
            hp = _gated_out(y, z, ssd_norm[j], w_out, hp, groups=groups)
            outs["ssm_p"].append(h_fin.reshape(bp, ssd_heads, ssd_hd, n_state))
            outs["conv_p"].append(xbc.reshape(bp, lp, cdim)[:, lp - (ssd_conv_w.shape[1] - 1):])

            z, xbc, dt = in_proj(hs)
            conv_state = state_conv[j]
            xc = _ssd_conv_step(xbc, jnp.swapaxes(conv_state, 0, 1), ssd_conv_w[j], ssd_conv_b[j])
            y, h_new = _ssd_step(state_ssm[j].reshape(bs, di, n_state), xc[:, :di], dt[:, :ssd_heads],
                                 a_neg, ssd_d[j], xc[:, di:di + gn], xc[:, di + gn:],
                                 groups=groups, hd=ssd_hd)
            hs = _gated_out(y, z, ssd_norm[j], w_out, hs, groups=groups)
            outs["ssm_s"].append(h_new.reshape(bs, ssd_heads, ssd_hd, n_state))
            outs["conv_s"].append(jnp.concatenate([conv_state[:, 1:], xbc[:, None]], axis=1))
            i_ssd += 1
        else:
            j = i_pool
            w_grp = pool_w[j].astype(BF16)
            hp, pre = _pool(hp, g, w_grp, pool_scale[j], batch=bp)
            outs["pool_p"].append(pre[:, 16 - pool_past:])
            pool_state = state_pool[j]
            hs, xn_s = _pool_step(hs, jnp.swapaxes(pool_state, 0, 1), g, w_grp, pool_scale[j], pos=past)
            outs["pool_s"].append(jnp.concatenate([pool_state[:, 1:], xn_s[:, None]], axis=1))
            i_pool += 1
        w_up, w_down = mlp_w_up[layer].astype(BF16), mlp_w_down[layer].astype(BF16)
        hp = _mlp(hp, norm_mlp[layer], w_up, w_down)
        hs = _mlp(hs, norm_mlp[layer], w_up, w_down)
    y_prompt = _norm(hp, final_norm).reshape(bp, lp, d)
    y_sample = _norm(hs, final_norm).reshape(bs, ls, d)
    return (y_prompt, y_sample) + tuple(jnp.stack(outs[name]) for name in (
        "kp", "vp", "fp", "ks", "vs", "fs", "ssm_p", "conv_p", "ssm_s", "conv_s", "pool_p", "pool_s"))
```

```python
import functools

import jax
import jax.numpy as jnp
from jax import lax
from jax.experimental import pallas as pl
from jax.experimental.pallas import tpu as pltpu

F32 = jnp.float32
BF16 = jnp.bfloat16
EPS = 1e-6
NEG = -1e30
LANES = 128
SSD_CHUNK = 128
POOL_WINDOWS = (2, 4, 8, 16)
VMEM_LIMIT = 48 * 1024 * 1024
NT_DIMS = (((1,), (1,)), ((), ()))


def _params(*sem):
    return pltpu.CompilerParams(dimension_semantics=sem, vmem_limit_bytes=VMEM_LIMIT)


def _rms(x, g):
    return x * lax.rsqrt(jnp.mean(x * x, axis=-1, keepdims=True) + EPS) * g


def _softplus(x):
    return jnp.maximum(x, 0.0) + jnp.log1p(jnp.exp(-jnp.abs(x)))


def _sigmoid(x):
    return 1.0 / (1.0 + jnp.exp(-x))


def _split3(x):
    hi = x.astype(BF16)
    rest = x - hi.astype(F32)
    mid = rest.astype(BF16)
    return hi, mid, (rest - mid.astype(F32)).astype(BF16)


def _dot_onehot(x, sel):
    sel = sel.astype(BF16)
    return sum(jnp.dot(piece, sel, preferred_element_type=F32) for piece in _split3(x))


def _onehot_dot(sel, x):
    sel = sel.astype(BF16)
    return sum(jnp.dot(sel, piece, preferred_element_type=F32) for piece in _split3(x))


def _norm_linear_kernel(x_ref, g_ref, w_ref, b_ref, *refs, act, outs):
    out_refs, xn_ref = refs[:-1], refs[-1]

    @pl.when(pl.program_id(1) == 0)
    def _():
        xn_ref[...] = _rms(x_ref[...], g_ref[...]).astype(BF16)

    y = jnp.dot(xn_ref[...], w_ref[...], preferred_element_type=F32) + b_ref[...]
    if act == "log_sigmoid":
        y = -_softplus(-y)
    elif act == "softplus":
        y = _softplus(y)
    for o_ref, (dtype, scale) in zip(out_refs, outs):
        o_ref[...] = (y * scale).astype(dtype)


def _norm_linear(x, g, w, b=None, *, act=None, outs=((F32, 1.0),), tm=512, tn=1024):
    m, k = x.shape
    n = w.shape[1]
    tm, tn = min(tm, m), min(tn, n)
    if b is None:
        b = jnp.zeros((n,), F32)
    res = pl.pallas_call(
        functools.partial(_norm_linear_kernel, act=act, outs=outs),
        grid=(m // tm, n // tn),
        in_specs=[
            pl.BlockSpec((tm, k), lambda i, j: (i, 0)),
            pl.BlockSpec((1, k), lambda i, j: (0, 0)),
            pl.BlockSpec((k, tn), lambda i, j: (0, j)),
            pl.BlockSpec((1, tn), lambda i, j: (0, j)),
        ],
        out_specs=[pl.BlockSpec((tm, tn), lambda i, j: (i, j)) for _ in outs],
        out_shape=[jax.ShapeDtypeStruct((m, n), d) for d, _ in outs],
        scratch_shapes=[pltpu.VMEM((tm, k), BF16)],
        compiler_params=_params("parallel", "arbitrary"),
        name="norm_linear",
    )(x, g.reshape(1, k), w, b.reshape(1, n))
    return res


def _linear_res_kernel(a_ref, w_ref, r_ref, o_ref):
    o_ref[...] = r_ref[...] + jnp.dot(a_ref[...], w_ref[...], preferred_element_type=F32)


def _linear_res(a, w, res, *, tm=512):
    m, k = a.shape
    n = w.shape[1]
    tm = min(tm, m)
    return pl.pallas_call(
        _linear_res_kernel,
        grid=(m // tm,),
        in_specs=[
            pl.BlockSpec((tm, k), lambda i: (i, 0)),
            pl.BlockSpec((k, n), lambda i: (0, 0)),
            pl.BlockSpec((tm, n), lambda i: (i, 0)),
        ],
        out_specs=pl.BlockSpec((tm, n), lambda i: (i, 0)),
        out_shape=jax.ShapeDtypeStruct((m, n), F32),
        compiler_params=_params("parallel"),
        name="linear_res",
    )(a, w, res)


def _mlp_kernel(x_ref, g_ref, wu_ref, wd_ref, o_ref, xn_ref, acc_ref):
    f = pl.program_id(1)

    @pl.when(f == 0)
    def _():
        xn_ref[...] = _rms(x_ref[...], g_ref[...]).astype(BF16)
        acc_ref[...] = jnp.zeros_like(acc_ref)

    h = jnp.maximum(jnp.dot(xn_ref[...], wu_ref[...], preferred_element_type=F32), 0.0)
    acc_ref[...] += jnp.dot((h * h).astype(BF16), wd_ref[...], preferred_element_type=F32)

    @pl.when(f == pl.num_programs(1) - 1)
    def _():
        o_ref[...] = x_ref[...] + acc_ref[...]


def _mlp(x, g, w_up, w_down, *, tm=1024, tf=512):
    m, d = x.shape
    ff = w_up.shape[1]
    tm = min(tm, m)
    return pl.pallas_call(
        _mlp_kernel,
        grid=(m // tm, ff // tf),
        in_specs=[
            pl.BlockSpec((tm, d), lambda i, f: (i, 0)),
            pl.BlockSpec((1, d), lambda i, f: (0, 0)),
            pl.BlockSpec((d, tf), lambda i, f: (0, f)),
            pl.BlockSpec((tf, d), lambda i, f: (f, 0)),
        ],
        out_specs=pl.BlockSpec((tm, d), lambda i, f: (i, 0)),
        out_shape=jax.ShapeDtypeStruct((m, d), F32),
        scratch_shapes=[pltpu.VMEM((tm, d), BF16), pltpu.VMEM((tm, d), F32)],
        compiler_params=_params("parallel", "arbitrary"),
        name="mlp",
    )(x, g.reshape(1, d), w_up, w_down)


def _norm_kernel(x_ref, g_ref, o_ref):
    o_ref[...] = _rms(x_ref[...], g_ref[...])


def _norm(x, g, *, tm=1024):
    m, d = x.shape
    tm = min(tm, m)
    return pl.pallas_call(
        _norm_kernel,
        grid=(m // tm,),
        in_specs=[pl.BlockSpec((tm, d), lambda i: (i, 0)), pl.BlockSpec((1, d), lambda i: (0, 0))],
        out_specs=pl.BlockSpec((tm, d), lambda i: (i, 0)),
        out_shape=jax.ShapeDtypeStruct((m, d), F32),
        compiler_params=_params("parallel"),
        name="final_norm",
    )(x, g.reshape(1, d))


def _cumsum_kernel(x_ref, o_ref):
    rows, length = x_ref.shape
    r = lax.broadcasted_iota(jnp.int32, (LANES, LANES), 0)
    c = lax.broadcasted_iota(jnp.int32, (LANES, LANES), 1)
    upper = (r <= c).astype(F32)

    carry = jnp.zeros((rows, 1), F32)
    for j in range(length // LANES):
        sl = slice(j * LANES, (j + 1) * LANES)
        cs = _dot_onehot(x_ref[:, sl], upper) + carry
        o_ref[:, sl] = cs
        carry = cs[:, LANES - 1:LANES]


def _cumsum_lanes(x):
    b, rows, length = x.shape
    return pl.pallas_call(
        _cumsum_kernel,
        grid=(b,),
        in_specs=[pl.BlockSpec((None, rows, length), lambda i: (i, 0, 0))],
        out_specs=pl.BlockSpec((None, rows, length), lambda i: (i, 0, 0)),
        out_shape=jax.ShapeDtypeStruct(x.shape, F32),
        compiler_params=_params("parallel"),
        name="cumsum_lanes",
    )(x)


def _fox_proj_kernel(x_ref, g_ref, wq_ref, wkt_ref, wvt_ref, wft_ref, bf_ref,
                     q_ref, kt_ref, kt16_ref, vt_ref, vt16_ref, ft_ref, *, scale):
    xn = _rms(x_ref[...], g_ref[...]).astype(BF16)
    q_ref[...] = (jnp.dot(xn, wq_ref[...], preferred_element_type=F32) * scale).astype(q_ref.dtype)
    kt = lax.dot_general(wkt_ref[...], xn, NT_DIMS, preferred_element_type=F32)
    kt_ref[...] = kt
    kt16_ref[...] = kt.astype(BF16)
    vt = lax.dot_general(wvt_ref[...], xn, NT_DIMS, preferred_element_type=F32)
    vt_ref[...] = vt
    vt16_ref[...] = vt.astype(BF16)
    ft = lax.dot_general(wft_ref[...], xn, NT_DIMS, preferred_element_type=F32) + bf_ref[...]
    ft_ref[...] = -_softplus(-ft[:ft_ref.shape[0]])


def _fox_proj(x, g, w_qkv, w_f, b_f, *, batch, tm=512):
    t, d = x.shape
    heads = w_f.shape[1]
    seq = t // batch
    tm = min(tm, seq)
    nl = seq // tm
    wq = w_qkv[:, :d].astype(BF16)
    wkt = w_qkv[:, d:2 * d].T.astype(BF16)
    wvt = w_qkv[:, 2 * d:].T.astype(BF16)
    wft = jnp.zeros((LANES, d), BF16).at[:heads].set(w_f.T.astype(BF16))
    bf = jnp.zeros((LANES, 1), F32).at[:heads, 0].set(b_f)
    const = lambda shape: pl.BlockSpec(shape, lambda b, i: (0, 0))
    col = lambda rows: pl.BlockSpec((None, rows, tm), lambda b, i: (b, 0, i))
    tshape = lambda rows, dt: jax.ShapeDtypeStruct((batch, rows, seq), dt)
    return pl.pallas_call(
        functools.partial(_fox_proj_kernel, scale=(d // heads) ** -0.5),
        grid=(batch, nl),
        in_specs=[pl.BlockSpec((tm, d), lambda b, i: (b * nl + i, 0)), const((1, d)), const((d, d)),
                  const((d, d)), const((d, d)), const((LANES, d)), const((LANES, 1))],
        out_specs=[pl.BlockSpec((tm, d), lambda b, i: (b * nl + i, 0)),
                   col(d), col(d), col(d), col(d), col(heads)],
        out_shape=[jax.ShapeDtypeStruct((t, d), BF16), tshape(d, F32), tshape(d, BF16),
                   tshape(d, F32), tshape(d, BF16), tshape(heads, F32)],
        compiler_params=_params("parallel", "parallel"),
        name="fox_proj",
    )(x, g.reshape(1, d), wq, wkt, wvt, wft, bf)


def _fox_attn_kernel(q_ref, kt_ref, vt_ref, ct_ref, o_ref, va_ref, vb_ref, m_ref, acc_ref, *, tq, hd):
    pair, i = pl.program_id(1), pl.program_id(2)

    @pl.when(i == 0)
    def _():
        row = lax.broadcasted_iota(jnp.int32, vt_ref.shape, 0)
        vt = vt_ref[...].astype(F32)
        va_ref[...] = jnp.where(row < hd, vt, jnp.where(row == hd, 1.0, 0.0)).astype(BF16)
        vb_ref[...] = jnp.where(row >= hd, vt, jnp.where(row == 0, 1.0, 0.0)).astype(BF16)

    q2 = q_ref[...]
    lo = lax.broadcasted_iota(jnp.int32, (tq, 2 * hd), 1) < hd
    zero = jnp.zeros_like(q2)
    qs = (jnp.where(lo, q2, zero), jnp.where(lo, zero, q2))
    vrefs = (va_ref, vb_ref)
    m_ref[...] = jnp.full_like(m_ref, NEG)
    acc_ref[...] = jnp.zeros_like(acc_ref)

    def step(j, masked):
        sl = pl.ds(pl.multiple_of(j * tq, tq), tq)
        kt = kt_ref[:, sl]
        for h in range(2):
            s = jnp.dot(qs[h], kt, preferred_element_type=F32) - ct_ref[pl.ds(2 * pair + h, 1), sl]
            if masked:
                keep = (lax.broadcasted_iota(jnp.int32, (tq, tq), 1)
                        <= lax.broadcasted_iota(jnp.int32, (tq, tq), 0))
                s = jnp.where(keep, s, NEG)
            m_prev = m_ref[h]
            m_new = jnp.maximum(m_prev, jnp.max(s, axis=1, keepdims=True))
            alpha = jnp.exp(m_prev - m_new)
            p = jnp.exp(s - m_new[:, :1]).astype(BF16)
            acc_ref[h] = alpha * acc_ref[h] + lax.dot_general(p, vrefs[h][:, sl], NT_DIMS,
                                                              preferred_element_type=F32)
            m_ref[h] = m_new

    def full_step(j, carry):
        step(j, False)
        return carry

    lax.fori_loop(0, i, full_step, 0)
    step(i, True)
    acc_a, acc_b = acc_ref[0], acc_ref[1]
    out = jnp.where(lo, acc_a / acc_a[:, hd:hd + 1], acc_b / acc_b[:, 0:1])
    o_ref[...] = out.astype(o_ref.dtype)


def _fox_attn(q, kt, vt, ct, *, batch, tq=1024):
    t, d = q.shape
    heads = ct.shape[1]
    hd = d // heads
    seq = t // batch
    tq = min(tq, seq)
    nq = seq // tq
    pair_cols = pl.BlockSpec((None, 2 * hd, seq), lambda b, p, i: (b, p, 0))
    return pl.pallas_call(
        functools.partial(_fox_attn_kernel, tq=tq, hd=hd),
        grid=(batch, heads // 2, nq),
        in_specs=[
            pl.BlockSpec((tq, 2 * hd), lambda b, p, i: (b * nq + i, p)),
            pair_cols, pair_cols,
            pl.BlockSpec((None, heads, seq), lambda b, p, i: (b, 0, 0)),
        ],
        out_specs=pl.BlockSpec((tq, 2 * hd), lambda b, p, i: (b * nq + i, p)),
        out_shape=jax.ShapeDtypeStruct((t, d), BF16),
        scratch_shapes=[pltpu.VMEM((2 * hd, seq), BF16), pltpu.VMEM((2 * hd, seq), BF16),
                        pltpu.VMEM((2, tq, LANES), F32), pltpu.VMEM((2, tq, 2 * hd), F32)],
        compiler_params=_params("parallel", "parallel", "arbitrary"),
        name="fox_attn",
    )(q, kt, vt, ct)


def _fox_decode_kernel(pt_ref, q_ref, kn_ref, vn_ref, fn_ref, *refs, pages, heads, hd, scale):
    del pt_ref
    k_refs, v_refs, f_refs = refs[:pages], refs[pages:2 * pages], refs[2 * pages:3 * pages]
    o_ref, m_ref, l_ref, acc_ref, cs_ref = refs[3 * pages:]
    step = pl.program_id(1)
    d = heads * hd
    psz = k_refs[0].shape[1]

    @pl.when(step == 0)
    def _():
        m_ref[...] = jnp.full_like(m_ref, NEG)
        l_ref[...] = jnp.zeros_like(l_ref)
        acc_ref[...] = jnp.zeros_like(acc_ref)
        cs_ref[...] = jnp.zeros_like(cs_ref)

    own = (lax.broadcasted_iota(jnp.int32, (heads, d), 1) // hd
           == lax.broadcasted_iota(jnp.int32, (heads, d), 0))
    q_bd = jnp.where(own, q_ref[...] * scale, 0.0)
    q_bd16 = q_bd.astype(BF16)
    upper = (lax.broadcasted_iota(jnp.int32, (psz, psz), 0)
             <= lax.broadcasted_iota(jnp.int32, (psz, psz), 1)).astype(F32)
    base = cs_ref[...]
    logits = []
    for r in range(pages):
        c = _dot_onehot(f_refs[r][...], upper)
        s = jnp.dot(q_bd16, k_refs[r][...].astype(BF16), preferred_element_type=F32)
        logits.append(s - (c + base))
        base = base + c[:, psz - 1:psz]
    cs_ref[...] = base
    s = jnp.concatenate(logits, axis=1)
    m_prev = m_ref[...]
    m_new = jnp.maximum(m_prev, jnp.max(s, axis=1, keepdims=True))
    alpha = jnp.exp(m_prev - m_new)
    p = jnp.exp(s - m_new)
    l_ref[...] = alpha * l_ref[...] + jnp.sum(p, axis=1, keepdims=True)
    p16 = p.astype(BF16)
    pv = lax.dot_general(p16[:, :psz], v_refs[0][...].astype(BF16), NT_DIMS, preferred_element_type=F32)
    for r in range(1, pages):
        pv = pv + lax.dot_general(p16[:, r * psz:(r + 1) * psz], v_refs[r][...].astype(BF16), NT_DIMS,
                                  preferred_element_type=F32)
    acc_ref[...] = alpha * acc_ref[...] + pv
    m_ref[...] = m_new

    @pl.when(step == pl.num_programs(1) - 1)
    def _():
        s_new = (jnp.sum(q_bd16.astype(F32) * kn_ref[...], axis=1, keepdims=True)
                 - (cs_ref[...] + fn_ref[...]))
        m_prev = m_ref[...]
        m_new = jnp.maximum(m_prev, s_new)
        alpha = jnp.exp(m_prev - m_new)
        p_new = jnp.exp(s_new - m_new)
        l_fin = alpha * l_ref[...] + p_new
        acc = (alpha * acc_ref[...] + p_new * vn_ref[...]) / l_fin
        o_ref[...] = jnp.sum(jnp.where(own, acc, 0.0), axis=0, keepdims=True)


def _fox_decode(q, k_new, v_new, f_new, cache_kt, cache_vt, cache_ft, page_table, layer, *, heads, pages=8):
    rows, _, d = q.shape
    hd = d // heads
    psz = cache_kt.shape[3]
    n_pages = page_table.shape[1]
    pages = min(pages, n_pages)

    def page_map(r):
        return lambda b, s, pt: (layer, pt[b * n_pages + s * pages + r], 0, 0)

    row_spec = pl.BlockSpec((None, 1, d), lambda b, s, pt: (b, 0, 0))
    in_specs = [row_spec, row_spec, row_spec,
                pl.BlockSpec((None, heads, 1), lambda b, s, pt: (b, 0, 0))]
    in_specs += [pl.BlockSpec((None, None, d, psz), page_map(r)) for r in range(pages)]
    in_specs += [pl.BlockSpec((None, None, d, psz), page_map(r)) for r in range(pages)]
    in_specs += [pl.BlockSpec((None, None, heads, psz), page_map(r)) for r in range(pages)]
    return pl.pallas_call(
        functools.partial(_fox_decode_kernel, pages=pages, heads=heads, hd=hd, scale=hd ** -0.5),
        grid_spec=pltpu.PrefetchScalarGridSpec(
            num_scalar_prefetch=1,
            grid=(rows, n_pages // pages),
            in_specs=in_specs,
            out_specs=row_spec,
            scratch_shapes=[pltpu.VMEM((heads, 1), F32), pltpu.VMEM((heads, 1), F32),
                            pltpu.VMEM((heads, d), F32), pltpu.VMEM((heads, 1), F32)],
        ),
        out_shape=jax.ShapeDtypeStruct((rows, 1, d), F32),
        compiler_params=_params("parallel", "arbitrary"),
        name="fox_decode",
    )(page_table.reshape(-1), q, k_new, v_new, f_new,
      *([cache_kt] * pages), *([cache_vt] * pages), *([cache_ft] * pages))


def _conv_kernel(u_ref, w_ref, b_ref, xs_ref, bm_ref, cm_ref, ext_ref, *, taps):
    tl = u_ref.shape[0]
    pad = 8

    @pl.when(pl.program_id(1) == 0)
    def _():
        ext_ref[0:pad, :] = jnp.zeros((pad, ext_ref.shape[1]), F32)

    @pl.when(pl.program_id(1) > 0)
    def _():
        ext_ref[0:pad, :] = ext_ref[tl:tl + pad, :]

    u = u_ref[...]
    ext_ref[pad:pad + tl, :] = u
    y = w_ref[0:1, :] * ext_ref[pl.ds(pad - (taps - 1), tl), :]
    for j in range(1, taps - 1):
        y = y + w_ref[j:j + 1, :] * ext_ref[pl.ds(pad - (taps - 1 - j), tl), :]
    y = y + w_ref[taps - 1:taps, :] * u + b_ref[...]
    y = y * _sigmoid(y)
    di, gn = xs_ref.shape[1], bm_ref.shape[1]
    xs_ref[...] = y[:, :di]
    bm_ref[...] = y[:, di:di + gn]
    cm_ref[...] = y[:, di + gn:]


def _ssd_conv(xbc, w, b, *, batch, di, gn, tl=256):
    t, cdim = xbc.shape
    seq = t // batch
    tl = min(tl, seq)
    nl = seq // tl
    taps = w.shape[0]
    return pl.pallas_call(
        functools.partial(_conv_kernel, taps=taps),
        grid=(batch, nl),
        in_specs=[
            pl.BlockSpec((tl, cdim), lambda bb, l: (bb * nl + l, 0)),
            pl.BlockSpec((taps, cdim), lambda bb, l: (0, 0)),
            pl.BlockSpec((1, cdim), lambda bb, l: (0, 0)),
        ],
        out_specs=[
            pl.BlockSpec((tl, di), lambda bb, l: (bb * nl + l, 0)),
            pl.BlockSpec((tl, gn), lambda bb, l: (bb * nl + l, 0)),
            pl.BlockSpec((tl, gn), lambda bb, l: (bb * nl + l, 0)),
        ],
        out_shape=[jax.ShapeDtypeStruct((t, di), F32), jax.ShapeDtypeStruct((t, gn), F32),
                   jax.ShapeDtypeStruct((t, gn), F32)],
        scratch_shapes=[pltpu.VMEM((tl + 8, cdim), F32)],
        compiler_params=_params("parallel", "arbitrary"),
        name="ssd_conv",
    )(xbc, w, b.reshape(1, cdim))


def _conv_step_kernel(u_ref, st_ref, w_ref, b_ref, o_ref):
    taps = w_ref.shape[0]
    y = w_ref[taps - 1:taps, :] * u_ref[...] + b_ref[...]
    for j in range(taps - 1):
        y = y + w_ref[j:j + 1, :] * st_ref[j]
    o_ref[...] = y * _sigmoid(y)


def _ssd_conv_step(u, state_t, w, b):
    rows, cdim = u.shape
    return pl.pallas_call(
        _conv_step_kernel,
        out_shape=jax.ShapeDtypeStruct((rows, cdim), F32),
        compiler_params=pltpu.CompilerParams(vmem_limit_bytes=VMEM_LIMIT),
        name="ssd_conv_step",
    )(u, state_t, w, b.reshape(1, cdim))


def _ssd_scan_kernel(xs_ref, dt_ref, bm_ref, cm_ref, a_ref, dsk_ref, y_ref, st_ref, ht_ref, *, hpg, hd):
    grp, c = pl.program_id(1), pl.program_id(2)
    q = xs_ref.shape[0]
    width = hpg * hd

    @pl.when(c == 0)
    def _():
        ht_ref[...] = jnp.zeros_like(ht_ref)

    ri = lax.broadcasted_iota(jnp.int32, (q, q), 0)
    ci = lax.broadcasted_iota(jnp.int32, (q, q), 1)
    causal = ci <= ri
    tri = causal.astype(F32)
    dt = dt_ref[...]
    acum = _onehot_dot(tri, dt * a_ref[...])
    er = lax.broadcasted_iota(jnp.int32, (LANES, width), 0)
    ec = lax.broadcasted_iota(jnp.int32, (LANES, width), 1)
    expand = (er == grp * hpg + ec // hd).astype(F32)
    sr = lax.broadcasted_iota(jnp.int32, (LANES, LANES), 0)
    sc = lax.broadcasted_iota(jnp.int32, (LANES, LANES), 1)
    select = (sr == grp * hpg + sc).astype(F32)
    acum_x = _dot_onehot(acum, expand)
    dt_x = _dot_onehot(dt, expand)
    acum_g = _dot_onehot(acum, select)
    acum_gt = acum_g.T

    xs = xs_ref[...]
    xdt = xs * dt_x
    xdt_b = xdt.astype(BF16)
    bm = bm_ref[...].astype(BF16)
    cm = cm_ref[...].astype(BF16)
    cb = lax.dot_general(cm, bm, NT_DIMS, preferred_element_type=F32)
    lo = lax.broadcasted_iota(jnp.int32, (q, 2 * hd), 1) < hd
    parts = []
    for pr in range(hpg // 2):
        xp = xdt_b[:, pr * 2 * hd:(pr + 1) * 2 * hd]
        ys = []
        for r in (2 * pr, 2 * pr + 1):
            seg = acum_g[:, r:r + 1] - acum_gt[r:r + 1, :]
            mat = (cb * jnp.exp(jnp.where(causal, seg, NEG))).astype(BF16)
            ys.append(jnp.dot(mat, xp, preferred_element_type=F32))
        parts.append(jnp.where(lo, ys[0], ys[1]))
    y_diag = jnp.concatenate(parts, axis=1)

    ht = ht_ref[...]
    y_off = jnp.dot(cm, ht.astype(BF16), preferred_element_type=F32) * jnp.exp(acum_x)
    y_ref[...] = y_diag + y_off + xs * dsk_ref[...]

    a_last = acum_x[q - 1:q, :]
    xw = (xdt * jnp.exp(a_last - acum_x)).astype(BF16)
    st = jnp.dot(bm_ref[...].T.astype(BF16), xw, preferred_element_type=F32)
    ht_new = ht * jnp.exp(a_last) + st
    ht_ref[...] = ht_new

    @pl.when(c == pl.num_programs(2) - 1)
    def _():
        st_ref[...] = ht_new.T


def _ssd_scan(xs, dt, bm, cm, a_neg, d_skip, *, batch, groups, hd, chunk=SSD_CHUNK):
    t, di = xs.shape
    n = bm.shape[1] // groups
    seq = t // batch
    nc = seq // chunk
    width = di // groups
    hpg = width // hd
    heads = di // hd
    a_row = jnp.zeros((1, LANES), F32).at[0, :heads].set(a_neg)
    dsk = jnp.repeat(d_skip, hd).reshape(1, di)
    return pl.pallas_call(
        functools.partial(_ssd_scan_kernel, hpg=hpg, hd=hd),
        grid=(batch, groups, nc),
        in_specs=[
            pl.BlockSpec((chunk, width), lambda b, g, c: (b * nc + c, g)),
            pl.BlockSpec((chunk, LANES), lambda b, g, c: (b * nc + c, 0)),
            pl.BlockSpec((chunk, n), lambda b, g, c: (b * nc + c, g)),
            pl.BlockSpec((chunk, n), lambda b, g, c: (b * nc + c, g)),
            pl.BlockSpec((1, LANES), lambda b, g, c: (0, 0)),
            pl.BlockSpec((1, width), lambda b, g, c: (0, g)),
        ],
        out_specs=[
            pl.BlockSpec((chunk, width), lambda b, g, c: (b * nc + c, g)),
            pl.BlockSpec((None, width, n), lambda b, g, c: (b, g, 0)),
        ],
        out_shape=[jax.ShapeDtypeStruct((t, di), F32), jax.ShapeDtypeStruct((batch, di, n), F32)],
        scratch_shapes=[pltpu.VMEM((n, width), F32)],
        compiler_params=_params("parallel", "parallel", "arbitrary"),
        name="ssd_scan",
    )(xs, dt, bm, cm, a_row, dsk)


def _ssd_step_kernel(h_ref, xs_ref, dt_ref, a_ref, dsk_ref, bm_ref, cm_ref, y_ref, ho_ref, *, groups):
    rows = h_ref.shape[0]
    per = rows // groups
    xc, dtc = xs_ref[...], dt_ref[...]
    decay = jnp.exp(dtc * a_ref[...])
    xdt = xc * dtc
    for g in range(groups):
        sl = slice(g * per, (g + 1) * per)
        hn = h_ref[sl, :] * decay[sl] + xdt[sl] * bm_ref[g:g + 1, :]
        ho_ref[sl, :] = hn
        y_ref[sl, :] = (jnp.sum(hn * cm_ref[g:g + 1, :], axis=1, keepdims=True)
                        + xc[sl] * dsk_ref[sl, :])


def _ssd_step(h0, xs, dt, a_neg, d_skip, bm, cm, *, groups, hd):
    rows, di, n = h0.shape
    col = lambda v: v.reshape(rows, di, 1)
    a_col = jnp.repeat(a_neg, hd).reshape(di, 1)
    dsk = jnp.repeat(d_skip, hd).reshape(di, 1)
    row3 = lambda last: pl.BlockSpec((None, di, last), lambda b: (b, 0, 0))
    grp3 = pl.BlockSpec((None, groups, n), lambda b: (b, 0, 0))
    const = pl.BlockSpec((di, 1), lambda b: (0, 0))
    y, h_new = pl.pallas_call(
        functools.partial(_ssd_step_kernel, groups=groups),
        grid=(rows,),
        in_specs=[row3(n), row3(1), row3(1), const, const, grp3, grp3],
        out_specs=[row3(1), row3(n)],
        out_shape=[jax.ShapeDtypeStruct((rows, di, 1), F32), jax.ShapeDtypeStruct((rows, di, n), F32)],
        compiler_params=_params("parallel"),
        name="ssd_step",
    )(h0, col(xs), col(jnp.repeat(dt, hd, axis=1)), a_col, dsk,
      bm.reshape(rows, groups, n), cm.reshape(rows, groups, n))
    return y.reshape(rows, di), h_new


def _gated_out_kernel(y_ref, z_ref, ng_ref, w_ref, r_ref, o_ref, *, groups):
    z = z_ref[...]
    yg = y_ref[...] * (z * _sigmoid(z))
    per = yg.shape[1] // groups
    parts = []
    for g in range(groups):
        blk = yg[:, g * per:(g + 1) * per]
        parts.append(blk * lax.rsqrt(jnp.mean(blk * blk, axis=-1, keepdims=True) + EPS))
    yn = (jnp.concatenate(parts, axis=1) * ng_ref[...]).astype(BF16)
    o_ref[...] = r_ref[...] + jnp.dot(yn, w_ref[...], preferred_element_type=F32)


def _gated_out(y, z, ng, w, res, *, groups, tm=256):
    m, di = y.shape
    n = w.shape[1]
    tm = min(tm, m)
    return pl.pallas_call(
        functools.partial(_gated_out_kernel, groups=groups),
        grid=(m // tm,),
        in_specs=[
            pl.BlockSpec((tm, di), lambda i: (i, 0)),
            pl.BlockSpec((tm, di), lambda i: (i, 0)),
            pl.BlockSpec((1, di), lambda i: (0, 0)),
            pl.BlockSpec((di, n), lambda i: (0, 0)),
            pl.BlockSpec((tm, n), lambda i: (i, 0)),
        ],
        out_specs=pl.BlockSpec((tm, n), lambda i: (i, 0)),
        out_shape=jax.ShapeDtypeStruct((m, n), F32),
        compiler_params=_params("parallel"),
        name="ssd_gated_out",
    )(y, z, ng.reshape(1, di), w, res)


def _pool_kernel(x_ref, g_ref, w_ref, sc_ref, o_ref, pre_ref, ext_ref):
    tl, d = x_ref.shape
    pad = 16
    l = pl.program_id(1)

    @pl.when(l == 0)
    def _():
        ext_ref[0:pad, :] = jnp.zeros((pad, d), F32)

    @pl.when(l > 0)
    def _():
        ext_ref[0:pad, :] = ext_ref[tl:tl + pad, :]

    x = x_ref[...]
    xn = _rms(x, g_ref[...])
    ext_ref[pad:pad + tl, :] = xn
    pos = l * tl + lax.broadcasted_iota(jnp.int32, (tl, 1), 0)
    gd = d // len(POOL_WINDOWS)
    for gi, win in enumerate(POOL_WINDOWS):
        cols = slice(gi * gd, (gi + 1) * gd)
        tot = xn[:, cols]
        for j in range(1, win):
            tot = tot + ext_ref[pl.ds(pad - j, tl), cols]
        cnt = jnp.minimum(pos + 1, win).astype(F32)
        pooled = tot / cnt - xn[:, cols]
        mixed = jnp.dot(pooled.astype(BF16), w_ref[gi], preferred_element_type=F32)
        o_ref[:, cols] = x[:, cols] + mixed * sc_ref[:, cols]

    @pl.when(l == pl.num_programs(1) - 1)
    def _():
        pre_ref[...] = xn[tl - pad:, :]


def _pool(x, g, w_grp, scale, *, batch, tl=512):
    t, d = x.shape
    seq = t // batch
    tl = min(tl, seq)
    nl = seq // tl
    ng, gd = w_grp.shape[0], w_grp.shape[1]
    return pl.pallas_call(
        _pool_kernel,
        grid=(batch, nl),
        in_specs=[
            pl.BlockSpec((tl, d), lambda b, l: (b * nl + l, 0)),
            pl.BlockSpec((1, d), lambda b, l: (0, 0)),
            pl.BlockSpec((ng, gd, gd), lambda b, l: (0, 0, 0)),
            pl.BlockSpec((1, d), lambda b, l: (0, 0)),
        ],
        out_specs=[
            pl.BlockSpec((tl, d), lambda b, l: (b * nl + l, 0)),
            pl.BlockSpec((None, 16, d), lambda b, l: (b, 0, 0)),
        ],
        out_shape=[jax.ShapeDtypeStruct((t, d), F32), jax.ShapeDtypeStruct((batch, 16, d), F32)],
        scratch_shapes=[pltpu.VMEM((tl + 16, d), F32)],
        compiler_params=_params("parallel", "arbitrary"),
        name="pool",
    )(x, g.reshape(1, d), w_grp, scale.reshape(1, d))


def _pool_step_kernel(x_ref, st_ref, g_ref, w_ref, sc_ref, o_ref, xn_ref, *, pos):
    x = x_ref[...]
    xn = _rms(x, g_ref[...])
    xn_ref[...] = xn
    past = st_ref.shape[0]
    gd = x.shape[1] // len(POOL_WINDOWS)
    for gi, win in enumerate(POOL_WINDOWS):
        cols = slice(gi * gd, (gi + 1) * gd)
        tot = xn[:, cols]
        for j in range(1, win):
            tot = tot + st_ref[past - j][:, cols]
        pooled = tot / float(min(pos + 1, win)) - xn[:, cols]
        mixed = jnp.dot(pooled.astype(BF16), w_ref[gi], preferred_element_type=F32)
        o_ref[:, cols] = x[:, cols] + mixed * sc_ref[:, cols]


def _pool_step(x, state_t, g, w_grp, scale, *, pos):
    rows, d = x.shape
    return pl.pallas_call(
        functools.partial(_pool_step_kernel, pos=pos),
        out_shape=[jax.ShapeDtypeStruct((rows, d), F32), jax.ShapeDtypeStruct((rows, d), F32)],
        compiler_params=pltpu.CompilerParams(vmem_limit_bytes=VMEM_LIMIT),
        name="pool_step",
    )(x, state_t, g.reshape(1, d), w_grp, scale.reshape(1, d))


def _fox_proj_rows(h, g, w_qkv, w_f, b_f):
    d = h.shape[1]
    heads = w_f.shape[1]
    (qkv,) = _norm_linear(h, g, w_qkv.astype(BF16))
    wf = jnp.zeros((d, LANES), BF16).at[:, :heads].set(w_f.astype(BF16))
    bf = jnp.zeros((LANES,), F32).at[:heads].set(b_f)
    (logf,) = _norm_linear(h, g, wf, bf, act="log_sigmoid")
    return qkv[:, :d], qkv[:, d:2 * d], qkv[:, 2 * d:], logf[:, :heads]


def kernel(x_prompt, x_sample, cache_k, cache_v, cache_logf, page_table, state_ssm, state_conv, state_pool, norm_mix, norm_mlp, final_norm, fox_w_qkv, fox_w_f, fox_b_f, fox_w_o, ssd_w_in, ssd_conv_w, ssd_conv_b, ssd_dt_bias, ssd_a_log, ssd_d, ssd_norm, ssd_w_out, pool_w, pool_scale, mlp_w_up, mlp_w_down):
    bp, lp, d = x_prompt.shape
    bs, ls, _ = x_sample.shape
    assert ls == 1, "the sample group decodes one token per row"
    depth = norm_mix.shape[0]
    heads = fox_w_f.shape[-1]
    hd = d // heads
    n_layers, n_pool, psz = cache_k.shape[:3]
    past = page_table.shape[1] * psz
    ssd_heads = ssd_a_log.shape[1]
    di = ssd_norm.shape[1]
    cdim = ssd_conv_b.shape[1]
    n_state = state_ssm.shape[-1]
    ssd_hd = state_ssm.shape[-2]
    groups = (cdim - di) // (2 * n_state)
    gn = groups * n_state
    pool_past = state_pool.shape[2]

    hp = x_prompt.reshape(bp * lp, d)
    hs = x_sample.reshape(bs, d)
    cache_kt = jnp.transpose(cache_k, (0, 1, 3, 4, 2)).reshape(n_layers, n_pool, d, psz)
    cache_vt = jnp.transpose(cache_v, (0, 1, 3, 4, 2)).reshape(n_layers, n_pool, d, psz)
    cache_ft = jnp.swapaxes(cache_logf, 2, 3)
    outs = {name: [] for name in ("kp", "vp", "fp", "ks", "vs", "fs", "ssm_p", "conv_p", "ssm_s",
                                  "conv_s", "pool_p", "pool_s")}
    i_fox = i_ssd = i_pool = 0
    for layer in range(depth):
        kind = layer % 3
        g = norm_mix[layer]
        if kind == 0:
            j = i_fox
            w_o = fox_w_o[j].astype(BF16)
            q, kt, kt16, vt, vt16, ft = _fox_proj(hp, g, fox_w_qkv[j], fox_w_f[j], fox_b_f[j], batch=bp)
            ctx = _fox_attn(q, kt16, vt16, _cumsum_lanes(ft), batch=bp)
            hp = _linear_res(ctx, w_o, hp)
            outs["kp"].append(jnp.transpose(kt.reshape(bp, heads, hd, lp), (0, 3, 1, 2)))
            outs["vp"].append(jnp.transpose(vt.reshape(bp, heads, hd, lp), (0, 3, 1, 2)))
            outs["fp"].append(jnp.swapaxes(ft, 1, 2))

            q, k, v, logf = _fox_proj_rows(hs, g, fox_w_qkv[j], fox_w_f[j], fox_b_f[j])
            ctx = _fox_decode(q.reshape(bs, 1, d), k.reshape(bs, 1, d), v.reshape(bs, 1, d),
                              logf.reshape(bs, heads, 1), cache_kt, cache_vt, cache_ft, page_table, j,
                              heads=heads)
            hs = _linear_res(ctx.reshape(bs, d).astype(BF16), w_o, hs)
            outs["ks"].append(k.reshape(bs, 1, heads, hd))
            outs["vs"].append(v.reshape(bs, 1, heads, hd))
            outs["fs"].append(logf.reshape(bs, 1, heads))
            i_fox += 1
        elif kind == 1:
            j = i_ssd
            w_in = ssd_w_in[j]
            w_z = w_in[:, :di].astype(BF16)
            w_xbc = w_in[:, di:di + cdim].astype(BF16)
            w_dt = jnp.zeros((d, LANES), BF16).at[:, :ssd_heads].set(w_in[:, di + cdim:].astype(BF16))
            b_dt = jnp.zeros((LANES,), F32).at[:ssd_heads].set(ssd_dt_bias[j])
            a_neg = -jnp.exp(ssd_a_log[j])
            w_out = ssd_w_out[j].astype(BF16)

            def in_proj(h):
                (z,) = _norm_linear(h, g, w_z)
                (xbc,) = _norm_linear(h, g, w_xbc)
                (dt,) = _norm_linear(h, g, w_dt, b_dt, act="softplus")
                return z, xbc, dt

            z, xbc, dt = in_proj(hp)
            xs, bm, cm = _ssd_conv(xbc, ssd_conv_w[j], ssd_conv_b[j], batch=bp, di=di, gn=gn)
            y, h_fin = _ssd_scan(xs, dt, bm, cm, a_neg, ssd_d[j], batch=bp, groups=groups, hd=ssd_hd)
            hp = _gated_out(y, z, ssd_norm[j], w_out, hp, groups=groups)
            outs["ssm_p"].append(h_fin.reshape(bp, ssd_heads, ssd_hd, n_state))
            outs["conv_p"].append(xbc.reshape(bp, lp, cdim)[:, lp - (ssd_conv_w.shape[1] - 1):])

            z, xbc, dt = in_proj(hs)
            conv_state = state_conv[j]
            xc = _ssd_conv_step(xbc, jnp.swapaxes(conv_state, 0, 1), ssd_conv_w[j], ssd_conv_b[j])
            y, h_new = _ssd_step(state_ssm[j].reshape(bs, di, n_state), xc[:, :di], dt[:, :ssd_heads],
                                 a_neg, ssd_d[j], xc[:, di:di + gn], xc[:, di + gn:],
                                 groups=groups, hd=ssd_hd)
            hs = _gated_out(y, z, ssd_norm[j], w_out, hs, groups=groups)
            outs["ssm_s"].append(h_new.reshape(bs, ssd_heads, ssd_hd, n_state))
            outs["conv_s"].append(jnp.concatenate([conv_state[:, 1:], xbc[:, None]], axis=1))
            i_ssd += 1
        else:
            j = i_pool
            w_grp = pool_w[j].astype(BF16)
            hp, pre = _pool(hp, g, w_grp, pool_scale[j], batch=bp)
            outs["pool_p"].append(pre[:, 16 - pool_past:])
            pool_state = state_pool[j]
            hs, xn_s = _pool_step(hs, jnp.swapaxes(pool_state, 0, 1), g, w_grp, pool_scale[j], pos=past)
            outs["pool_s"].append(jnp.concatenate([pool_state[:, 1:], xn_s[:, None]], axis=1))
            i_pool += 1
        w_up, w_down = mlp_w_up[layer].astype(BF16), mlp_w_down[layer].astype(BF16)
        hp = _mlp(hp, norm_mlp[layer], w_up, w_down)
        hs = _mlp(hs, norm_mlp[layer], w_up, w_down)
    y_prompt = _norm(hp, final_norm).reshape(bp, lp, d)
    y_sample = _norm(hs, final_norm).reshape(bs, ls, d)
    return (y_prompt, y_sample) + tuple(jnp.stack(outs[name]) for name in (
        "kp", "vp", "fp", "ks", "vs", "fs", "ssm_p", "conv_p", "ssm_s", "conv_s", "pool_p", "pool_s"))
```

```python
import functools

import jax
import jax.numpy as jnp
from jax import lax
from jax.experimental import pallas as pl
from jax.experimental.pallas import tpu as pltpu

F32 = jnp.float32
BF16 = jnp.bfloat16
EPS = 1e-6
NEG = -1e30
LANES = 128
SSD_CHUNK = 128
ATTN_BLOCK = 1024
POOL_WINDOWS = (2, 4, 8, 16)
VMEM_LIMIT = 48 * 1024 * 1024
FUSED_VMEM_LIMIT = 58 * 1024 * 1024
NT_DIMS = (((1,), (1,)), ((), ()))


def _params(*sem):
    return pltpu.CompilerParams(dimension_semantics=sem, vmem_limit_bytes=VMEM_LIMIT)


def _rms(x, g):
    return x * lax.rsqrt(jnp.mean(x * x, axis=-1, keepdims=True) + EPS) * g


def _softplus(x):
    return jnp.maximum(x, 0.0) + jnp.log1p(jnp.exp(-jnp.abs(x)))


def _sigmoid(x):
    return 1.0 / (1.0 + jnp.exp(-x))


def _split3(x):
    hi = x.astype(BF16)
    rest = x - hi.astype(F32)
    mid = rest.astype(BF16)
    return hi, mid, (rest - mid.astype(F32)).astype(BF16)


def _dot_onehot(x, sel):
    sel = sel.astype(BF16)
    return sum(jnp.dot(piece, sel, preferred_element_type=F32) for piece in _split3(x))


def _onehot_dot(sel, x):
    sel = sel.astype(BF16)
    return sum(jnp.dot(sel, piece, preferred_element_type=F32) for piece in _split3(x))


def _norm_linear_kernel(x_ref, g_ref, w_ref, b_ref, *refs, act, outs):
    out_refs, xn_ref = refs[:-1], refs[-1]

    @pl.when(pl.program_id(1) == 0)
    def _():
        xn_ref[...] = _rms(x_ref[...], g_ref[...]).astype(BF16)

    y = jnp.dot(xn_ref[...], w_ref[...], preferred_element_type=F32) + b_ref[...]
    if act == "log_sigmoid":
        y = -_softplus(-y)
    elif act == "softplus":
        y = _softplus(y)
    for o_ref, (dtype, scale) in zip(out_refs, outs):
        o_ref[...] = (y * scale).astype(dtype)


def _norm_linear(x, g, w, b=None, *, act=None, outs=((F32, 1.0),), tm=512, tn=1024):
    m, k = x.shape
    n = w.shape[1]
    tm, tn = min(tm, m), min(tn, n)
    if b is None:
        b = jnp.zeros((n,), F32)
    res = pl.pallas_call(
        functools.partial(_norm_linear_kernel, act=act, outs=outs),
        grid=(m // tm, n // tn),
        in_specs=[
            pl.BlockSpec((tm, k), lambda i, j: (i, 0)),
            pl.BlockSpec((1, k), lambda i, j: (0, 0)),
            pl.BlockSpec((k, tn), lambda i, j: (0, j)),
            pl.BlockSpec((1, tn), lambda i, j: (0, j)),
        ],
        out_specs=[pl.BlockSpec((tm, tn), lambda i, j: (i, j)) for _ in outs],
        out_shape=[jax.ShapeDtypeStruct((m, n), d) for d, _ in outs],
        scratch_shapes=[pltpu.VMEM((tm, k), BF16)],
        compiler_params=_params("parallel", "arbitrary"),
        name="norm_linear",
    )(x, g.reshape(1, k), w, b.reshape(1, n))
    return res


def _linear_res_kernel(a_ref, w_ref, r_ref, o_ref):
    o_ref[...] = r_ref[...] + jnp.dot(a_ref[...], w_ref[...], preferred_element_type=F32)


def _linear_res(a, w, res, *, tm=512):
    m, k = a.shape
    n = w.shape[1]
    tm = min(tm, m)
    return pl.pallas_call(
        _linear_res_kernel,
        grid=(m // tm,),
        in_specs=[
            pl.BlockSpec((tm, k), lambda i: (i, 0)),
            pl.BlockSpec((k, n), lambda i: (0, 0)),
            pl.BlockSpec((tm, n), lambda i: (i, 0)),
        ],
        out_specs=pl.BlockSpec((tm, n), lambda i: (i, 0)),
        out_shape=jax.ShapeDtypeStruct((m, n), F32),
        compiler_params=_params("parallel"),
        name="linear_res",
    )(a, w, res)


def _mlp_kernel(x_ref, g_ref, wu_ref, wd_ref, o_ref, xn_ref, acc_ref):
    f = pl.program_id(1)

    @pl.when(f == 0)
    def _():
        xn_ref[...] = _rms(x_ref[...], g_ref[...]).astype(BF16)
        acc_ref[...] = jnp.zeros_like(acc_ref)

    h = jnp.maximum(jnp.dot(xn_ref[...], wu_ref[...], preferred_element_type=F32), 0.0)
    acc_ref[...] += jnp.dot((h * h).astype(BF16), wd_ref[...], preferred_element_type=F32)

    @pl.when(f == pl.num_programs(1) - 1)
    def _():
        o_ref[...] = x_ref[...] + acc_ref[...]


def _mlp(x, g, w_up, w_down, *, tm=1024, tf=1024):
    m, d = x.shape
    ff = w_up.shape[1]
    tm = min(tm, m)
    return pl.pallas_call(
        _mlp_kernel,
        grid=(m // tm, ff // tf),
        in_specs=[
            pl.BlockSpec((tm, d), lambda i, f: (i, 0)),
            pl.BlockSpec((1, d), lambda i, f: (0, 0)),
            pl.BlockSpec((d, tf), lambda i, f: (0, f)),
            pl.BlockSpec((tf, d), lambda i, f: (f, 0)),
        ],
        out_specs=pl.BlockSpec((tm, d), lambda i, f: (i, 0)),
        out_shape=jax.ShapeDtypeStruct((m, d), F32),
        scratch_shapes=[pltpu.VMEM((tm, d), BF16), pltpu.VMEM((tm, d), F32)],
        compiler_params=_params("parallel", "arbitrary"),
        name="mlp",
    )(x, g.reshape(1, d), w_up, w_down)


def _norm_kernel(x_ref, g_ref, o_ref):
    o_ref[...] = _rms(x_ref[...], g_ref[...])


def _norm(x, g, *, tm=1024):
    m, d = x.shape
    tm = min(tm, m)
    return pl.pallas_call(
        _norm_kernel,
        grid=(m // tm,),
        in_specs=[pl.BlockSpec((tm, d), lambda i: (i, 0)), pl.BlockSpec((1, d), lambda i: (0, 0))],
        out_specs=pl.BlockSpec((tm, d), lambda i: (i, 0)),
        out_shape=jax.ShapeDtypeStruct((m, d), F32),
        compiler_params=_params("parallel"),
        name="final_norm",
    )(x, g.reshape(1, d))


def _cumsum_kernel(x_ref, o_ref):
    rows, length = x_ref.shape
    r = lax.broadcasted_iota(jnp.int32, (LANES, LANES), 0)
    c = lax.broadcasted_iota(jnp.int32, (LANES, LANES), 1)
    upper = (r <= c).astype(F32)

    carry = jnp.zeros((rows, 1), F32)
    for j in range(length // LANES):
        sl = slice(j * LANES, (j + 1) * LANES)
        cs = _dot_onehot(x_ref[:, sl], upper) + carry
        o_ref[:, sl] = cs
        carry = cs[:, LANES - 1:LANES]


def _cumsum_lanes(x):
    b, rows, length = x.shape
    return pl.pallas_call(
        _cumsum_kernel,
        grid=(b,),
        in_specs=[pl.BlockSpec((None, rows, length), lambda i: (i, 0, 0))],
        out_specs=pl.BlockSpec((None, rows, length), lambda i: (i, 0, 0)),
        out_shape=jax.ShapeDtypeStruct(x.shape, F32),
        compiler_params=_params("parallel"),
        name="cumsum_lanes",
    )(x)


def _fox_proj_kernel(x_ref, g_ref, wq_ref, wkt_ref, wvt_ref, wft_ref, bf_ref,
                     q_ref, kt_ref, kt16_ref, vt_ref, vt16_ref, ft_ref, *, scale):
    xn = _rms(x_ref[...], g_ref[...]).astype(BF16)
    q_ref[...] = (jnp.dot(xn, wq_ref[...], preferred_element_type=F32) * scale).astype(q_ref.dtype)
    kt = lax.dot_general(wkt_ref[...], xn, NT_DIMS, preferred_element_type=F32)
    kt_ref[...] = kt
    kt16_ref[...] = kt.astype(BF16)
    vt = lax.dot_general(wvt_ref[...], xn, NT_DIMS, preferred_element_type=F32)
    vt_ref[...] = vt
    vt16_ref[...] = vt.astype(BF16)
    ft = lax.dot_general(wft_ref[...], xn, NT_DIMS, preferred_element_type=F32) + bf_ref[...]
    ft_ref[...] = -_softplus(-ft[:ft_ref.shape[0]])


def _fox_proj(x, g, w_qkv, w_f, b_f, *, batch, tm=512):
    t, d = x.shape
    heads = w_f.shape[1]
    seq = t // batch
    tm = min(tm, seq)
    nl = seq // tm
    wq = w_qkv[:, :d].astype(BF16)
    wkt = w_qkv[:, d:2 * d].T.astype(BF16)
    wvt = w_qkv[:, 2 * d:].T.astype(BF16)
    wft = jnp.zeros((LANES, d), BF16).at[:heads].set(w_f.T.astype(BF16))
    bf = jnp.zeros((LANES, 1), F32).at[:heads, 0].set(b_f)
    const = lambda shape: pl.BlockSpec(shape, lambda b, i: (0, 0))
    col = lambda rows: pl.BlockSpec((None, rows, tm), lambda b, i: (b, 0, i))
    tshape = lambda rows, dt: jax.ShapeDtypeStruct((batch, rows, seq), dt)
    return pl.pallas_call(
        functools.partial(_fox_proj_kernel, scale=(d // heads) ** -0.5),
        grid=(batch, nl),
        in_specs=[pl.BlockSpec((tm, d), lambda b, i: (b * nl + i, 0)), const((1, d)), const((d, d)),
                  const((d, d)), const((d, d)), const((LANES, d)), const((LANES, 1))],
        out_specs=[pl.BlockSpec((tm, d), lambda b, i: (b * nl + i, 0)),
                   col(d), col(d), col(d), col(d), col(heads)],
        out_shape=[jax.ShapeDtypeStruct((t, d), BF16), tshape(d, F32), tshape(d, BF16),
                   tshape(d, F32), tshape(d, BF16), tshape(heads, F32)],
        compiler_params=_params("parallel", "parallel"),
        name="fox_proj",
    )(x, g.reshape(1, d), wq, wkt, wvt, wft, bf)


def _attn_block(q_ref, kt_ref, vt_ref, ct_ref, o_ref, va_ref, vb_ref, m_ref, acc_ref, *, tq, hd,
                other_work=None):
    pair, i = pl.program_id(1), pl.program_id(2)

    @pl.when(i == 0)
    def _():
        row = lax.broadcasted_iota(jnp.int32, vt_ref.shape, 0)
        vt = vt_ref[...].astype(F32)
        va_ref[...] = jnp.where(row < hd, vt, jnp.where(row == hd, 1.0, 0.0)).astype(BF16)
        vb_ref[...] = jnp.where(row >= hd, vt, jnp.where(row == 0, 1.0, 0.0)).astype(BF16)

    q2 = q_ref[...]
    lo = lax.broadcasted_iota(jnp.int32, (tq, 2 * hd), 1) < hd
    zero = jnp.zeros_like(q2)
    qs = (jnp.where(lo, q2, zero), jnp.where(lo, zero, q2))
    vrefs = (va_ref, vb_ref)
    m_ref[...] = jnp.full_like(m_ref, NEG)
    acc_ref[...] = jnp.zeros_like(acc_ref)

    def step(j, masked):
        sl = pl.ds(pl.multiple_of(j * tq, tq), tq)
        kt = kt_ref[:, sl]
        for h in range(2):
            s = jnp.dot(qs[h], kt, preferred_element_type=F32) - ct_ref[pl.ds(2 * pair + h, 1), sl]
            if masked:
                keep = (lax.broadcasted_iota(jnp.int32, (tq, tq), 1)
                        <= lax.broadcasted_iota(jnp.int32, (tq, tq), 0))
                s = jnp.where(keep, s, NEG)
            m_prev = m_ref[h]
            m_new = jnp.maximum(m_prev, jnp.max(s, axis=1, keepdims=True))
            alpha = jnp.exp(m_prev - m_new)
            p = jnp.exp(s - m_new[:, :1]).astype(BF16)
            acc_ref[h] = alpha * acc_ref[h] + lax.dot_general(p, vrefs[h][:, sl], NT_DIMS,
                                                              preferred_element_type=F32)
            m_ref[h] = m_new

    def full_step(j, carry):
        step(j, False)
        return carry

    lax.fori_loop(0, i, full_step, 0)
    if other_work is not None:
        other_work()
    step(i, True)
    acc_a, acc_b = acc_ref[0], acc_ref[1]
    out = jnp.where(lo, acc_a / acc_a[:, hd:hd + 1], acc_b / acc_b[:, 0:1])
    o_ref[...] = out.astype(o_ref.dtype)


def _fox_attn_kernel(q_ref, kt_ref, vt_ref, ct_ref, o_ref, va_ref, vb_ref, m_ref, acc_ref, *, tq, hd):
    _attn_block(q_ref, kt_ref, vt_ref, ct_ref, o_ref, va_ref, vb_ref, m_ref, acc_ref, tq=tq, hd=hd)


def _fox_attn(q, kt, vt, ct, *, batch, tq):
    t, d = q.shape
    heads = ct.shape[1]
    hd = d // heads
    seq = t // batch
    nq = seq // tq
    pair_cols = pl.BlockSpec((None, 2 * hd, seq), lambda b, p, i: (b, p, 0))
    return pl.pallas_call(
        functools.partial(_fox_attn_kernel, tq=tq, hd=hd),
        grid=(batch, heads // 2, nq),
        in_specs=[
            pl.BlockSpec((tq, 2 * hd), lambda b, p, i: (b * nq + i, p)),
            pair_cols, pair_cols,
            pl.BlockSpec((None, heads, seq), lambda b, p, i: (b, 0, 0)),
        ],
        out_specs=pl.BlockSpec((tq, 2 * hd), lambda b, p, i: (b * nq + i, p)),
        out_shape=jax.ShapeDtypeStruct((t, d), BF16),
        scratch_shapes=[pltpu.VMEM((2 * hd, seq), BF16), pltpu.VMEM((2 * hd, seq), BF16),
                        pltpu.VMEM((2, tq, LANES), F32), pltpu.VMEM((2, tq, 2 * hd), F32)],
        compiler_params=_params("parallel", "parallel", "arbitrary"),
        name="fox_attn",
    )(q, kt, vt, ct)


def _decode_block(step, last, q_ref, kn_ref, vn_ref, fn_ref, k_refs, v_refs, f_refs,
                  o_ref, m_ref, l_ref, acc_ref, cs_ref, *, heads, hd, scale):
    pages = len(k_refs)
    d = heads * hd
    psz = k_refs[0].shape[1]

    @pl.when(step == 0)
    def _():
        m_ref[...] = jnp.full_like(m_ref, NEG)
        l_ref[...] = jnp.zeros_like(l_ref)
        acc_ref[...] = jnp.zeros_like(acc_ref)
        cs_ref[...] = jnp.zeros_like(cs_ref)

    own = (lax.broadcasted_iota(jnp.int32, (heads, d), 1) // hd
           == lax.broadcasted_iota(jnp.int32, (heads, d), 0))
    q_bd = jnp.where(own, q_ref[...] * scale, 0.0)
    q_bd16 = q_bd.astype(BF16)
    upper = (lax.broadcasted_iota(jnp.int32, (psz, psz), 0)
             <= lax.broadcasted_iota(jnp.int32, (psz, psz), 1)).astype(F32)
    base = cs_ref[...]
    logits = []
    for r in range(pages):
        c = _dot_onehot(f_refs[r][...], upper)
        s = jnp.dot(q_bd16, k_refs[r][...].astype(BF16), preferred_element_type=F32)
        logits.append(s - (c + base))
        base = base + c[:, psz - 1:psz]
    cs_ref[...] = base
    s = jnp.concatenate(logits, axis=1)
    m_prev = m_ref[...]
    m_new = jnp.maximum(m_prev, jnp.max(s, axis=1, keepdims=True))
    alpha = jnp.exp(m_prev - m_new)
    p = jnp.exp(s - m_new)
    l_ref[...] = alpha * l_ref[...] + jnp.sum(p, axis=1, keepdims=True)
    p16 = p.astype(BF16)
    pv = lax.dot_general(p16[:, :psz], v_refs[0][...].astype(BF16), NT_DIMS, preferred_element_type=F32)
    for r in range(1, pages):
        pv = pv + lax.dot_general(p16[:, r * psz:(r + 1) * psz], v_refs[r][...].astype(BF16), NT_DIMS,
                                  preferred_element_type=F32)
    acc_ref[...] = alpha * acc_ref[...] + pv
    m_ref[...] = m_new

    @pl.when(last)
    def _():
        s_new = (jnp.sum(q_bd16.astype(F32) * kn_ref[...], axis=1, keepdims=True)
                 - (cs_ref[...] + fn_ref[...]))
        m_prev = m_ref[...]
        m_new = jnp.maximum(m_prev, s_new)
        alpha = jnp.exp(m_prev - m_new)
        p_new = jnp.exp(s_new - m_new)
        l_fin = alpha * l_ref[...] + p_new
        acc = (alpha * acc_ref[...] + p_new * vn_ref[...]) / l_fin
        o_ref[...] = jnp.sum(jnp.where(own, acc, 0.0), axis=0, keepdims=True)


def _fox_decode_kernel(pt_ref, q_ref, kn_ref, vn_ref, fn_ref, *refs, pages, heads, hd, scale):
    del pt_ref
    k_refs, v_refs, f_refs = refs[:pages], refs[pages:2 * pages], refs[2 * pages:3 * pages]
    step = pl.program_id(1)
    _decode_block(step, step == pl.num_programs(1) - 1, q_ref, kn_ref, vn_ref, fn_ref, k_refs, v_refs,
                  f_refs, *refs[3 * pages:], heads=heads, hd=hd, scale=scale)


def _decode_scratch(heads, d):
    return [pltpu.VMEM((heads, 1), F32), pltpu.VMEM((heads, 1), F32), pltpu.VMEM((heads, d), F32),
            pltpu.VMEM((heads, 1), F32)]


def _fox_decode(q, k_new, v_new, f_new, cache_kt, cache_vt, cache_ft, page_table, layer, *, heads, pages=8):
    rows, _, d = q.shape
    hd = d // heads
    psz = cache_kt.shape[3]
    n_pages = page_table.shape[1]
    pages = min(pages, n_pages)

    def page_map(r):
        return lambda b, s, pt: (layer, pt[b * n_pages + s * pages + r], 0, 0)

    row_spec = pl.BlockSpec((None, 1, d), lambda b, s, pt: (b, 0, 0))
    in_specs = [row_spec, row_spec, row_spec,
                pl.BlockSpec((None, heads, 1), lambda b, s, pt: (b, 0, 0))]
    in_specs += [pl.BlockSpec((None, None, d, psz), page_map(r)) for r in range(pages)]
    in_specs += [pl.BlockSpec((None, None, d, psz), page_map(r)) for r in range(pages)]
    in_specs += [pl.BlockSpec((None, None, heads, psz), page_map(r)) for r in range(pages)]
    return pl.pallas_call(
        functools.partial(_fox_decode_kernel, pages=pages, heads=heads, hd=hd, scale=hd ** -0.5),
        grid_spec=pltpu.PrefetchScalarGridSpec(
            num_scalar_prefetch=1,
            grid=(rows, n_pages // pages),
            in_specs=in_specs,
            out_specs=row_spec,
            scratch_shapes=_decode_scratch(heads, d),
        ),
        out_shape=jax.ShapeDtypeStruct((rows, 1, d), F32),
        compiler_params=_params("parallel", "arbitrary"),
        name="fox_decode",
    )(page_table.reshape(-1), q, k_new, v_new, f_new,
      *([cache_kt] * pages), *([cache_vt] * pages), *([cache_ft] * pages))


def _fox_attn_decode_kernel(pt_ref, q_ref, kt_ref, vt_ref, ct_ref, qd_ref, kn_ref, vn_ref, fn_ref, *refs,
                            pages, tq, heads, hd, scale):
    del pt_ref
    k_refs, v_refs, f_refs = refs[:pages], refs[pages:2 * pages], refs[2 * pages:3 * pages]
    o_ref, od_ref, va_ref, vb_ref, m_ref, acc_ref = refs[3 * pages:3 * pages + 6]
    i = pl.program_id(2)

    def decode():
        _decode_block(i, i == pl.num_programs(2) - 1, qd_ref, kn_ref, vn_ref, fn_ref, k_refs, v_refs,
                      f_refs, od_ref, *refs[3 * pages + 6:], heads=heads, hd=hd, scale=scale)

    _attn_block(q_ref, kt_ref, vt_ref, ct_ref, o_ref, va_ref, vb_ref, m_ref, acc_ref, tq=tq, hd=hd,
                other_work=decode)


def _fox_attn_decode(q, kt, vt, ct, qd, k_new, v_new, f_new, cache_kt, cache_vt, cache_ft, page_table,
                     layer, *, batch, tq):
    t, d = q.shape
    heads = ct.shape[1]
    hd = d // heads
    pairs = heads // 2
    seq = t // batch
    nq = seq // tq
    rows = qd.shape[0]
    psz = cache_kt.shape[3]
    n_pages = page_table.shape[1]
    pages = n_pages // nq
    assert rows == batch * pairs and n_pages == nq * pages

    def page_map(r):
        return lambda b, p, i, pt: (layer, pt[(b * pairs + p) * n_pages + i * pages + r], 0, 0)

    pair_cols = pl.BlockSpec((None, 2 * hd, seq), lambda b, p, i, pt: (b, p, 0))
    row_spec = pl.BlockSpec((None, 1, d), lambda b, p, i, pt: (b * pairs + p, 0, 0))
    in_specs = [pl.BlockSpec((tq, 2 * hd), lambda b, p, i, pt: (b * nq + i, p)), pair_cols, pair_cols,
                pl.BlockSpec((None, heads, seq), lambda b, p, i, pt: (b, 0, 0)),
                row_spec, row_spec, row_spec,
                pl.BlockSpec((None, heads, 1), lambda b, p, i, pt: (b * pairs + p, 0, 0))]
    in_specs += [pl.BlockSpec((None, None, d, psz), page_map(r)) for r in range(pages)]
    in_specs += [pl.BlockSpec((None, None, d, psz), page_map(r)) for r in range(pages)]
    in_specs += [pl.BlockSpec((None, None, heads, psz), page_map(r)) for r in range(pages)]
    return pl.pallas_call(
        functools.partial(_fox_attn_decode_kernel, pages=pages, tq=tq, heads=heads, hd=hd, scale=hd ** -0.5),
        grid_spec=pltpu.PrefetchScalarGridSpec(
            num_scalar_prefetch=1,
            grid=(batch, pairs, nq),
            in_specs=in_specs,
            out_specs=[pl.BlockSpec((tq, 2 * hd), lambda b, p, i, pt: (b * nq + i, p)), row_spec],
            scratch_shapes=[pltpu.VMEM((2 * hd, seq), BF16), pltpu.VMEM((2 * hd, seq), BF16),
                            pltpu.VMEM((2, tq, LANES), F32), pltpu.VMEM((2, tq, 2 * hd), F32)]
            + _decode_scratch(heads, d),
        ),
        out_shape=[jax.ShapeDtypeStruct((t, d), BF16), jax.ShapeDtypeStruct((rows, 1, d), F32)],
        compiler_params=pltpu.CompilerParams(dimension_semantics=("parallel", "parallel", "arbitrary"),
                                             vmem_limit_bytes=FUSED_VMEM_LIMIT),
        name="fox_attn_decode",
    )(page_table.reshape(-1), q, kt, vt, ct, qd, k_new, v_new, f_new,
      *([cache_kt] * pages), *([cache_vt] * pages), *([cache_ft] * pages))


def _conv_kernel(u_ref, w_ref, b_ref, xs_ref, bm_ref, cm_ref, ext_ref, *, taps):
    tl = u_ref.shape[0]
    pad = 8

    @pl.when(pl.program_id(1) == 0)
    def _():
        ext_ref[0:pad, :] = jnp.zeros((pad, ext_ref.shape[1]), F32)

    @pl.when(pl.program_id(1) > 0)
    def _():
        ext_ref[0:pad, :] = ext_ref[tl:tl + pad, :]

    u = u_ref[...]
    ext_ref[pad:pad + tl, :] = u
    y = w_ref[0:1, :] * ext_ref[pl.ds(pad - (taps - 1), tl), :]
    for j in range(1, taps - 1):
        y = y + w_ref[j:j + 1, :] * ext_ref[pl.ds(pad - (taps - 1 - j), tl), :]
    y = y + w_ref[taps - 1:taps, :] * u + b_ref[...]
    y = y * _sigmoid(y)
    di, gn = xs_ref.shape[1], bm_ref.shape[1]
    xs_ref[...] = y[:, :di]
    bm_ref[...] = y[:, di:di + gn]
    cm_ref[...] = y[:, di + gn:]


def _ssd_conv(xbc, w, b, *, batch, di, gn, tl=256):
    t, cdim = xbc.shape
    seq = t // batch
    tl = min(tl, seq)
    nl = seq // tl
    taps = w.shape[0]
    return pl.pallas_call(
        functools.partial(_conv_kernel, taps=taps),
        grid=(batch, nl),
        in_specs=[
            pl.BlockSpec((tl, cdim), lambda bb, l: (bb * nl + l, 0)),
            pl.BlockSpec((taps, cdim), lambda bb, l: (0, 0)),
            pl.BlockSpec((1, cdim), lambda bb, l: (0, 0)),
        ],
        out_specs=[
            pl.BlockSpec((tl, di), lambda bb, l: (bb * nl + l, 0)),
            pl.BlockSpec((tl, gn), lambda bb, l: (bb * nl + l, 0)),
            pl.BlockSpec((tl, gn), lambda bb, l: (bb * nl + l, 0)),
        ],
        out_shape=[jax.ShapeDtypeStruct((t, di), F32), jax.ShapeDtypeStruct((t, gn), F32),
                   jax.ShapeDtypeStruct((t, gn), F32)],
        scratch_shapes=[pltpu.VMEM((tl + 8, cdim), F32)],
        compiler_params=_params("parallel", "arbitrary"),
        name="ssd_conv",
    )(xbc, w, b.reshape(1, cdim))


def _conv_step_kernel(u_ref, st_ref, w_ref, b_ref, o_ref):
    taps = w_ref.shape[0]
    y = w_ref[taps - 1:taps, :] * u_ref[...] + b_ref[...]
    for j in range(taps - 1):
        y = y + w_ref[j:j + 1, :] * st_ref[j]
    o_ref[...] = y * _sigmoid(y)


def _ssd_conv_step(u, state_t, w, b):
    rows, cdim = u.shape
    return pl.pallas_call(
        _conv_step_kernel,
        out_shape=jax.ShapeDtypeStruct((rows, cdim), F32),
        compiler_params=pltpu.CompilerParams(vmem_limit_bytes=VMEM_LIMIT),
        name="ssd_conv_step",
    )(u, state_t, w, b.reshape(1, cdim))


def _ssd_scan_kernel(xs_ref, dt_ref, bm_ref, cm_ref, a_ref, dsk_ref, y_ref, st_ref, ht_ref, *, groups, hd):
    c = pl.program_id(1)
    q, di = xs_ref.shape
    n = bm_ref.shape[1] // groups
    width = di // groups
    hpg = width // hd

    @pl.when(c == 0)
    def _():
        ht_ref[...] = jnp.zeros_like(ht_ref)

    ri = lax.broadcasted_iota(jnp.int32, (q, q), 0)
    ci = lax.broadcasted_iota(jnp.int32, (q, q), 1)
    causal = ci <= ri
    dt = dt_ref[...]
    acum = _onehot_dot(causal.astype(F32), dt * a_ref[...])
    acum_t = acum.T
    lo = lax.broadcasted_iota(jnp.int32, (q, 2 * hd), 1) < hd
    er = lax.broadcasted_iota(jnp.int32, (LANES, width), 0)
    ec = lax.broadcasted_iota(jnp.int32, (LANES, width), 1)
    for grp in range(groups):
        cols = slice(grp * width, (grp + 1) * width)
        ncols = slice(grp * n, (grp + 1) * n)
        expand = (er == grp * hpg + ec // hd).astype(F32)
        acum_x = _dot_onehot(acum, expand)
        dt_x = _dot_onehot(dt, expand)
        xs = xs_ref[:, cols]
        xdt = xs * dt_x
        xdt_b = xdt.astype(BF16)
        bm = bm_ref[:, ncols].astype(BF16)
        cm = cm_ref[:, ncols].astype(BF16)
        cb = lax.dot_general(cm, bm, NT_DIMS, preferred_element_type=F32)
        parts = []
        for pr in range(hpg // 2):
            xp = xdt_b[:, pr * 2 * hd:(pr + 1) * 2 * hd]
            ys = []
            for r in (grp * hpg + 2 * pr, grp * hpg + 2 * pr + 1):
                seg = acum[:, r:r + 1] - acum_t[r:r + 1, :]
                mat = (cb * jnp.exp(jnp.where(causal, seg, NEG))).astype(BF16)
                ys.append(jnp.dot(mat, xp, preferred_element_type=F32))
            parts.append(jnp.where(lo, ys[0], ys[1]))
        y_diag = jnp.concatenate(parts, axis=1)

        ht = ht_ref[:, cols]
        y_off = jnp.dot(cm, ht.astype(BF16), preferred_element_type=F32) * jnp.exp(acum_x)
        y_ref[:, cols] = y_diag + y_off + xs * dsk_ref[:, cols]

        a_last = acum_x[q - 1:q, :]
        xw = (xdt * jnp.exp(a_last - acum_x)).astype(BF16)
        st = jnp.dot(bm_ref[:, ncols].T.astype(BF16), xw, preferred_element_type=F32)
        ht_new = ht * jnp.exp(a_last) + st
        ht_ref[:, cols] = ht_new

        @pl.when(c == pl.num_programs(1) - 1)
        def _():
            st_ref[cols, :] = ht_new.T


def _ssd_scan(xs, dt, bm, cm, a_neg, d_skip, *, batch, groups, hd, chunk=SSD_CHUNK):
    t, di = xs.shape
    gn = bm.shape[1]
    n = gn // groups
    seq = t // batch
    nc = seq // chunk
    heads = di // hd
    a_row = jnp.zeros((1, LANES), F32).at[0, :heads].set(a_neg)
    dsk = jnp.repeat(d_skip, hd).reshape(1, di)
    rows = lambda width: pl.BlockSpec((chunk, width), lambda b, c: (b * nc + c, 0))
    const = lambda width: pl.BlockSpec((1, width), lambda b, c: (0, 0))
    return pl.pallas_call(
        functools.partial(_ssd_scan_kernel, groups=groups, hd=hd),
        grid=(batch, nc),
        in_specs=[rows(di), rows(LANES), rows(gn), rows(gn), const(LANES), const(di)],
        out_specs=[rows(di), pl.BlockSpec((None, di, n), lambda b, c: (b, 0, 0))],
        out_shape=[jax.ShapeDtypeStruct((t, di), F32), jax.ShapeDtypeStruct((batch, di, n), F32)],
        scratch_shapes=[pltpu.VMEM((n, di), F32)],
        compiler_params=_params("parallel", "arbitrary"),
        name="ssd_scan",
    )(xs, dt, bm, cm, a_row, dsk)


def _ssd_step_kernel(h_ref, xs_ref, dt_ref, a_ref, dsk_ref, bm_ref, cm_ref, y_ref, ho_ref, *, groups):
    rows = h_ref.shape[0]
    per = rows // groups
    xc, dtc = xs_ref[...], dt_ref[...]
    decay = jnp.exp(dtc * a_ref[...])
    xdt = xc * dtc
    for g in range(groups):
        sl = slice(g * per, (g + 1) * per)
        hn = h_ref[sl, :] * decay[sl] + xdt[sl] * bm_ref[g:g + 1, :]
        ho_ref[sl, :] = hn
        y_ref[sl, :] = (jnp.sum(hn * cm_ref[g:g + 1, :], axis=1, keepdims=True)
                        + xc[sl] * dsk_ref[sl, :])


def _ssd_step(h0, xs, dt, a_neg, d_skip, bm, cm, *, groups, hd):
    rows, di, n = h0.shape
    col = lambda v: v.reshape(rows, di, 1)
    a_col = jnp.repeat(a_neg, hd).reshape(di, 1)
    dsk = jnp.repeat(d_skip, hd).reshape(di, 1)
    row3 = lambda last: pl.BlockSpec((None, di, last), lambda b: (b, 0, 0))
    grp3 = pl.BlockSpec((None, groups, n), lambda b: (b, 0, 0))
    const = pl.BlockSpec((di, 1), lambda b: (0, 0))
    y, h_new = pl.pallas_call(
        functools.partial(_ssd_step_kernel, groups=groups),
        grid=(rows,),
        in_specs=[row3(n), row3(1), row3(1), const, const, grp3, grp3],
        out_specs=[row3(1), row3(n)],
        out_shape=[jax.ShapeDtypeStruct((rows, di, 1), F32), jax.ShapeDtypeStruct((rows, di, n), F32)],
        compiler_params=_params("parallel"),
        name="ssd_step",
    )(h0, col(xs), col(jnp.repeat(dt, hd, axis=1)), a_col, dsk,
      bm.reshape(rows, groups, n), cm.reshape(rows, groups, n))
    return y.reshape(rows, di), h_new


def _gated_out_kernel(y_ref, z_ref, ng_ref, w_ref, r_ref, o_ref, *, groups):
    z = z_ref[...]
    yg = y_ref[...] * (z * _sigmoid(z))
    per = yg.shape[1] // groups
    parts = []
    for g in range(groups):
        blk = yg[:, g * per:(g + 1) * per]
        parts.append(blk * lax.rsqrt(jnp.mean(blk * blk, axis=-1, keepdims=True) + EPS))
    yn = (jnp.concatenate(parts, axis=1) * ng_ref[...]).astype(BF16)
    o_ref[...] = r_ref[...] + jnp.dot(yn, w_ref[...], preferred_element_type=F32)


def _gated_out(y, z, ng, w, res, *, groups, tm=256):
    m, di = y.shape
    n = w.shape[1]
    tm = min(tm, m)
    return pl.pallas_call(
        functools.partial(_gated_out_kernel, groups=groups),
        grid=(m // tm,),
        in_specs=[
            pl.BlockSpec((tm, di), lambda i: (i, 0)),
            pl.BlockSpec((tm, di), lambda i: (i, 0)),
            pl.BlockSpec((1, di), lambda i: (0, 0)),
            pl.BlockSpec((di, n), lambda i: (0, 0)),
            pl.BlockSpec((tm, n), lambda i: (i, 0)),
        ],
        out_specs=pl.BlockSpec((tm, n), lambda i: (i, 0)),
        out_shape=jax.ShapeDtypeStruct((m, n), F32),
        compiler_params=_params("parallel"),
        name="ssd_gated_out",
    )(y, z, ng.reshape(1, di), w, res)


def _pool_kernel(x_ref, g_ref, w_ref, sc_ref, o_ref, pre_ref, ext_ref):
    tl, d = x_ref.shape
    pad = 16
    l = pl.program_id(1)

    @pl.when(l == 0)
    def _():
        ext_ref[0:pad, :] = jnp.zeros((pad, d), F32)

    @pl.when(l > 0)
    def _():
        ext_ref[0:pad, :] = ext_ref[tl:tl + pad, :]

    x = x_ref[...]
    xn = _rms(x, g_ref[...])
    ext_ref[pad:pad + tl, :] = xn
    pos = l * tl + lax.broadcasted_iota(jnp.int32, (tl, 1), 0)
    gd = d // len(POOL_WINDOWS)
    for gi, win in enumerate(POOL_WINDOWS):
        cols = slice(gi * gd, (gi + 1) * gd)
        tot = xn[:, cols]
        for j in range(1, win):
            tot = tot + ext_ref[pl.ds(pad - j, tl), cols]
        cnt = jnp.minimum(pos + 1, win).astype(F32)
        pooled = tot / cnt - xn[:, cols]
        mixed = jnp.dot(pooled.astype(BF16), w_ref[gi], preferred_element_type=F32)
        o_ref[:, cols] = x[:, cols] + mixed * sc_ref[:, cols]

    @pl.when(l == pl.num_programs(1) - 1)
    def _():
        pre_ref[...] = xn[tl - pad:, :]


def _pool(x, g, w_grp, scale, *, batch, tl=512):
    t, d = x.shape
    seq = t // batch
    tl = min(tl, seq)
    nl = seq // tl
    ng, gd = w_grp.shape[0], w_grp.shape[1]
    return pl.pallas_call(
        _pool_kernel,
        grid=(batch, nl),
        in_specs=[
            pl.BlockSpec((tl, d), lambda b, l: (b * nl + l, 0)),
            pl.BlockSpec((1, d), lambda b, l: (0, 0)),
            pl.BlockSpec((ng, gd, gd), lambda b, l: (0, 0, 0)),
            pl.BlockSpec((1, d), lambda b, l: (0, 0)),
        ],
        out_specs=[
            pl.BlockSpec((tl, d), lambda b, l: (b * nl + l, 0)),
            pl.BlockSpec((None, 16, d), lambda b, l: (b, 0, 0)),
        ],
        out_shape=[jax.ShapeDtypeStruct((t, d), F32), jax.ShapeDtypeStruct((batch, 16, d), F32)],
        scratch_shapes=[pltpu.VMEM((tl + 16, d), F32)],
        compiler_params=_params("parallel", "arbitrary"),
        name="pool",
    )(x, g.reshape(1, d), w_grp, scale.reshape(1, d))


def _pool_step_kernel(x_ref, st_ref, g_ref, w_ref, sc_ref, o_ref, xn_ref, *, pos):
    x = x_ref[...]
    xn = _rms(x, g_ref[...])
    xn_ref[...] = xn
    past = st_ref.shape[0]
    gd = x.shape[1] // len(POOL_WINDOWS)
    for gi, win in enumerate(POOL_WINDOWS):
        cols = slice(gi * gd, (gi + 1) * gd)
        tot = xn[:, cols]
        for j in range(1, win):
            tot = tot + st_ref[past - j][:, cols]
        pooled = tot / float(min(pos + 1, win)) - xn[:, cols]
        mixed = jnp.dot(pooled.astype(BF16), w_ref[gi], preferred_element_type=F32)
        o_ref[:, cols] = x[:, cols] + mixed * sc_ref[:, cols]


def _pool_step(x, state_t, g, w_grp, scale, *, pos):
    rows, d = x.shape
    return pl.pallas_call(
        functools.partial(_pool_step_kernel, pos=pos),
        out_shape=[jax.ShapeDtypeStruct((rows, d), F32), jax.ShapeDtypeStruct((rows, d), F32)],
        compiler_params=pltpu.CompilerParams(vmem_limit_bytes=VMEM_LIMIT),
        name="pool_step",
    )(x, state_t, g.reshape(1, d), w_grp, scale.reshape(1, d))


def _fox_proj_rows(h, g, w_qkv, w_f, b_f):
    d = h.shape[1]
    heads = w_f.shape[1]
    (qkv,) = _norm_linear(h, g, w_qkv.astype(BF16))
    wf = jnp.zeros((d, LANES), BF16).at[:, :heads].set(w_f.astype(BF16))
    bf = jnp.zeros((LANES,), F32).at[:heads].set(b_f)
    (logf,) = _norm_linear(h, g, wf, bf, act="log_sigmoid")
    return qkv[:, :d], qkv[:, d:2 * d], qkv[:, 2 * d:], logf[:, :heads]


def kernel(x_prompt, x_sample, cache_k, cache_v, cache_logf, page_table, state_ssm, state_conv, state_pool, norm_mix, norm_mlp, final_norm, fox_w_qkv, fox_w_f, fox_b_f, fox_w_o, ssd_w_in, ssd_conv_w, ssd_conv_b, ssd_dt_bias, ssd_a_log, ssd_d, ssd_norm, ssd_w_out, pool_w, pool_scale, mlp_w_up, mlp_w_down):
    bp, lp, d = x_prompt.shape
    bs, ls, _ = x_sample.shape
    assert ls == 1, "the sample group decodes one token per row"
    depth = norm_mix.shape[0]
    heads = fox_w_f.shape[-1]
    hd = d // heads
    n_layers, n_pool, psz = cache_k.shape[:3]
    past = page_table.shape[1] * psz
    ssd_heads = ssd_a_log.shape[1]
    di = ssd_norm.shape[1]
    cdim = ssd_conv_b.shape[1]
    n_state = state_ssm.shape[-1]
    ssd_hd = state_ssm.shape[-2]
    groups = (cdim - di) // (2 * n_state)
    gn = groups * n_state
    pool_past = state_pool.shape[2]

    hp = x_prompt.reshape(bp * lp, d)
    hs = x_sample.reshape(bs, d)
    cache_kt = jnp.transpose(cache_k, (0, 1, 3, 4, 2)).reshape(n_layers, n_pool, d, psz)
    cache_vt = jnp.transpose(cache_v, (0, 1, 3, 4, 2)).reshape(n_layers, n_pool, d, psz)
    cache_ft = jnp.swapaxes(cache_logf, 2, 3)
    outs = {name: [] for name in ("kp", "vp", "fp", "ks", "vs", "fs", "ssm_p", "conv_p", "ssm_s",
                                  "conv_s", "pool_p", "pool_s")}
    i_fox = i_ssd = i_pool = 0
    for layer in range(depth):
        kind = layer % 3
        g = norm_mix[layer]
        if kind == 0:
            j = i_fox
            w_o = fox_w_o[j].astype(BF16)
            q, kt, kt16, vt, vt16, ft = _fox_proj(hp, g, fox_w_qkv[j], fox_w_f[j], fox_b_f[j], batch=bp)
            ct = _cumsum_lanes(ft)
            qd, k, v, logf = _fox_proj_rows(hs, g, fox_w_qkv[j], fox_w_f[j], fox_b_f[j])
            dec_args = (qd.reshape(bs, 1, d), k.reshape(bs, 1, d), v.reshape(bs, 1, d),
                        logf.reshape(bs, heads, 1), cache_kt, cache_vt, cache_ft, page_table, j)
            tq = min(ATTN_BLOCK, lp)
            if bs == bp * (heads // 2) and page_table.shape[1] % (lp // tq) == 0:
                ctx, ctx_s = _fox_attn_decode(q, kt16, vt16, ct, *dec_args, batch=bp, tq=tq)
            else:
                ctx = _fox_attn(q, kt16, vt16, ct, batch=bp, tq=tq)
                ctx_s = _fox_decode(*dec_args, heads=heads)
            hp = _linear_res(ctx, w_o, hp)
            outs["kp"].append(jnp.transpose(kt.reshape(bp, heads, hd, lp), (0, 3, 1, 2)))
            outs["vp"].append(jnp.transpose(vt.reshape(bp, heads, hd, lp), (0, 3, 1, 2)))
            outs["fp"].append(jnp.swapaxes(ft, 1, 2))
            hs = _linear_res(ctx_s.reshape(bs, d).astype(BF16), w_o, hs)
            outs["ks"].append(k.reshape(bs, 1, heads, hd))
            outs["vs"].append(v.reshape(bs, 1, heads, hd))
            outs["fs"].append(logf.reshape(bs, 1, heads))
            i_fox += 1
        elif kind == 1:
            j = i_ssd
            w_in = ssd_w_in[j]
            w_z = w_in[:, :di].astype(BF16)
            w_xbc = w_in[:, di:di + cdim].astype(BF16)
            w_dt = jnp.zeros((d, LANES), BF16).at[:, :ssd_heads].set(w_in[:, di + cdim:].astype(BF16))
            b_dt = jnp.zeros((LANES,), F32).at[:ssd_heads].set(ssd_dt_bias[j])
            a_neg = -jnp.exp(ssd_a_log[j])
            w_out = ssd_w_out[j].astype(BF16)

            def in_proj(h):
                (z,) = _norm_linear(h, g, w_z)
                (xbc,) = _norm_linear(h, g, w_xbc)
                (dt,) = _norm_linear(h, g, w_dt, b_dt, act="softplus")
                return z, xbc, dt

            z, xbc, dt = in_proj(hp)
            xs, bm, cm = _ssd_conv(xbc, ssd_conv_w[j], ssd_conv_b[j], batch=bp, di=di, gn=gn)
            y, h_fin = _ssd_scan(xs, dt, bm, cm, a_neg, ssd_d[j], batch=bp, groups=groups, hd=ssd_hd)
            hp = _gated_out(y, z, ssd_norm[j], w_out, hp, groups=groups)
            outs["ssm_p"].append(h_fin.reshape(bp, ssd_heads, ssd_hd, n_state))
            outs["conv_p"].append(xbc.reshape(bp, lp, cdim)[:, lp - (ssd_conv_w.shape[1] - 1):])

            z, xbc, dt = in_proj(hs)
            conv_state = state_conv[j]
            xc = _ssd_conv_step(xbc, jnp.swapaxes(conv_state, 0, 1), ssd_conv_w[j], ssd_conv_b[j])
            y, h_new = _ssd_step(state_ssm[j].reshape(bs, di, n_state), xc[:, :di], dt[:, :ssd_heads],
                                 a_neg, ssd_d[j], xc[:, di:di + gn], xc[:, di + gn:],
                                 groups=groups, hd=ssd_hd)
            hs = _gated_out(y, z, ssd_norm[j], w_out, hs, groups=groups)
            outs["ssm_s"].append(h_new.reshape(bs, ssd_heads, ssd_hd, n_state))
            outs["conv_s"].append(jnp.concatenate([conv_state[:, 1:], xbc[:, None]], axis=1))
            i_ssd += 1
        else:
            j = i_pool
            w_grp = pool_w[j].astype(BF16)
            hp, pre = _pool(hp, g, w_grp, pool_scale[j], batch=bp)
            outs["pool_p"].append(pre[:, 16 - pool_past:])
            pool_state = state_pool[j]
            hs, xn_s = _pool_step(hs, jnp.swapaxes(pool_state, 0, 1), g, w_grp, pool_scale[j], pos=past)
            outs["pool_s"].append(jnp.concatenate([pool_state[:, 1:], xn_s[:, None]], axis=1))
            i_pool += 1
        w_up, w_down = mlp_w_up[layer].astype(BF16), mlp_w_down[layer].astype(BF16)
        hp = _mlp(hp, norm_mlp[layer], w_up, w_down)
        hs = _mlp(hs, norm_mlp[layer], w_up, w_down)
    y_prompt = _norm(hp, final_norm).reshape(bp, lp, d)
    y_sample = _norm(hs, final_norm).reshape(bs, ls, d)
    return (y_prompt, y_sample) + tuple(jnp.stack(outs[name]) for name in (
        "kp", "vp", "fp", "ks", "vs", "fs", "ssm_p", "conv_p", "ssm_s", "conv_s", "pool_p", "pool_s"))
```

```python
import functools

import jax
import jax.numpy as jnp
from jax import lax
from jax.experimental import pallas as pl
from jax.experimental.pallas import tpu as pltpu

F32 = jnp.float32
BF16 = jnp.bfloat16
EPS = 1e-6
NEG = -1e30
LANES = 128
SSD_CHUNK = 128
ATTN_BLOCK = 1024
POOL_WINDOWS = (2, 4, 8, 16)
VMEM_LIMIT = 48 * 1024 * 1024
FUSED_VMEM_LIMIT = 58 * 1024 * 1024
NT_DIMS = (((1,), (1,)), ((), ()))


def _params(*sem):
    return pltpu.CompilerParams(dimension_semantics=sem, vmem_limit_bytes=VMEM_LIMIT)


def _rms(x, g):
    return x * lax.rsqrt(jnp.mean(x * x, axis=-1, keepdims=True) + EPS) * g


def _softplus(x):
    return jnp.maximum(x, 0.0) + jnp.log1p(jnp.exp(-jnp.abs(x)))


def _sigmoid(x):
    return 1.0 / (1.0 + jnp.exp(-x))


def _split3(x):
    hi = x.astype(BF16)
    rest = x - hi.astype(F32)
    mid = rest.astype(BF16)
    return hi, mid, (rest - mid.astype(F32)).astype(BF16)


def _dot_onehot(x, sel):
    sel = sel.astype(BF16)
    return sum(jnp.dot(piece, sel, preferred_element_type=F32) for piece in _split3(x))


def _onehot_dot(sel, x):
    sel = sel.astype(BF16)
    return sum(jnp.dot(sel, piece, preferred_element_type=F32) for piece in _split3(x))


def _norm_linear_kernel(x_ref, g_ref, w_ref, b_ref, *refs, act, outs):
    out_refs, xn_ref = refs[:-1], refs[-1]

    @pl.when(pl.program_id(1) == 0)
    def _():
        xn_ref[...] = _rms(x_ref[...], g_ref[...]).astype(BF16)

    y = jnp.dot(xn_ref[...], w_ref[...], preferred_element_type=F32) + b_ref[...]
    if act == "log_sigmoid":
        y = -_softplus(-y)
    elif act == "softplus":
        y = _softplus(y)
    for o_ref, (dtype, scale) in zip(out_refs, outs):
        o_ref[...] = (y * scale).astype(dtype)


def _norm_linear(x, g, w, b=None, *, act=None, outs=((F32, 1.0),), tm=512, tn=1024):
    m, k = x.shape
    n = w.shape[1]
    tm, tn = min(tm, m), min(tn, n)
    if b is None:
        b = jnp.zeros((n,), F32)
    res = pl.pallas_call(
        functools.partial(_norm_linear_kernel, act=act, outs=outs),
        grid=(m // tm, n // tn),
        in_specs=[
            pl.BlockSpec((tm, k), lambda i, j: (i, 0)),
            pl.BlockSpec((1, k), lambda i, j: (0, 0)),
            pl.BlockSpec((k, tn), lambda i, j: (0, j)),
            pl.BlockSpec((1, tn), lambda i, j: (0, j)),
        ],
        out_specs=[pl.BlockSpec((tm, tn), lambda i, j: (i, j)) for _ in outs],
        out_shape=[jax.ShapeDtypeStruct((m, n), d) for d, _ in outs],
        scratch_shapes=[pltpu.VMEM((tm, k), BF16)],
        compiler_params=_params("parallel", "arbitrary"),
        name="norm_linear",
    )(x, g.reshape(1, k), w, b.reshape(1, n))
    return res


def _linear_res_kernel(a_ref, w_ref, r_ref, o_ref):
    o_ref[...] = r_ref[...] + jnp.dot(a_ref[...], w_ref[...], preferred_element_type=F32)


def _linear_res(a, w, res, *, tm=512):
    m, k = a.shape
    n = w.shape[1]
    tm = min(tm, m)
    return pl.pallas_call(
        _linear_res_kernel,
        grid=(m // tm,),
        in_specs=[
            pl.BlockSpec((tm, k), lambda i: (i, 0)),
            pl.BlockSpec((k, n), lambda i: (0, 0)),
            pl.BlockSpec((tm, n), lambda i: (i, 0)),
        ],
        out_specs=pl.BlockSpec((tm, n), lambda i: (i, 0)),
        out_shape=jax.ShapeDtypeStruct((m, n), F32),
        compiler_params=_params("parallel"),
        name="linear_res",
    )(a, w, res)


def _mlp_kernel(x_ref, g_ref, wu_ref, wd_ref, *refs, final_norm):
    o_ref, xn_ref, acc_ref = refs[-3:]
    f = pl.program_id(1)

    @pl.when(f == 0)
    def _():
        xn_ref[...] = _rms(x_ref[...], g_ref[...]).astype(BF16)
        acc_ref[...] = jnp.zeros_like(acc_ref)

    h = jnp.maximum(jnp.dot(xn_ref[...], wu_ref[...], preferred_element_type=F32), 0.0)
    acc_ref[...] += jnp.dot((h * h).astype(BF16), wd_ref[...], preferred_element_type=F32)

    @pl.when(f == pl.num_programs(1) - 1)
    def _():
        y = x_ref[...] + acc_ref[...]
        o_ref[...] = _rms(y, refs[0][...]) if final_norm else y


def _mlp(x, g, w_up, w_down, final_g=None, *, tm=1024, tf=1024):
    m, d = x.shape
    ff = w_up.shape[1]
    tm = min(tm, m)
    gain = pl.BlockSpec((1, d), lambda i, f: (0, 0))
    final = [] if final_g is None else [final_g.reshape(1, d)]
    return pl.pallas_call(
        functools.partial(_mlp_kernel, final_norm=final_g is not None),
        grid=(m // tm, ff // tf),
        in_specs=[
            pl.BlockSpec((tm, d), lambda i, f: (i, 0)),
            gain,
            pl.BlockSpec((d, tf), lambda i, f: (0, f)),
            pl.BlockSpec((tf, d), lambda i, f: (f, 0)),
        ] + [gain] * len(final),
        out_specs=pl.BlockSpec((tm, d), lambda i, f: (i, 0)),
        out_shape=jax.ShapeDtypeStruct((m, d), F32),
        scratch_shapes=[pltpu.VMEM((tm, d), BF16), pltpu.VMEM((tm, d), F32)],
        compiler_params=_params("parallel", "arbitrary"),
        name="mlp",
    )(x, g.reshape(1, d), w_up, w_down, *final)


def _cumsum_kernel(x_ref, o_ref):
    rows, length = x_ref.shape
    r = lax.broadcasted_iota(jnp.int32, (LANES, LANES), 0)
    c = lax.broadcasted_iota(jnp.int32, (LANES, LANES), 1)
    upper = (r <= c).astype(F32)

    carry = jnp.zeros((rows, 1), F32)
    for j in range(length // LANES):
        sl = slice(j * LANES, (j + 1) * LANES)
        cs = _dot_onehot(x_ref[:, sl], upper) + carry
        o_ref[:, sl] = cs
        carry = cs[:, LANES - 1:LANES]


def _cumsum_lanes(x):
    b, rows, length = x.shape
    return pl.pallas_call(
        _cumsum_kernel,
        grid=(b,),
        in_specs=[pl.BlockSpec((None, rows, length), lambda i: (i, 0, 0))],
        out_specs=pl.BlockSpec((None, rows, length), lambda i: (i, 0, 0)),
        out_shape=jax.ShapeDtypeStruct(x.shape, F32),
        compiler_params=_params("parallel"),
        name="cumsum_lanes",
    )(x)


def _fox_proj_kernel(x_ref, g_ref, wq_ref, wkt_ref, wvt_ref, wft_ref, bf_ref,
                     q_ref, kt_ref, kt16_ref, vt_ref, vt16_ref, ft_ref, *, scale):
    xn = _rms(x_ref[...], g_ref[...]).astype(BF16)
    q_ref[...] = (jnp.dot(xn, wq_ref[...], preferred_element_type=F32) * scale).astype(q_ref.dtype)
    kt = lax.dot_general(wkt_ref[...], xn, NT_DIMS, preferred_element_type=F32)
    kt_ref[...] = kt
    kt16_ref[...] = kt.astype(BF16)
    vt = lax.dot_general(wvt_ref[...], xn, NT_DIMS, preferred_element_type=F32)
    vt_ref[...] = vt
    vt16_ref[...] = vt.astype(BF16)
    ft = lax.dot_general(wft_ref[...], xn, NT_DIMS, preferred_element_type=F32) + bf_ref[...]
    ft_ref[...] = -_softplus(-ft[:ft_ref.shape[0]])


def _fox_proj(x, g, w_qkv, w_f, b_f, *, batch, tm=512):
    t, d = x.shape
    heads = w_f.shape[1]
    seq = t // batch
    tm = min(tm, seq)
    nl = seq // tm
    wq = w_qkv[:, :d].astype(BF16)
    wkt = w_qkv[:, d:2 * d].T.astype(BF16)
    wvt = w_qkv[:, 2 * d:].T.astype(BF16)
    wft = jnp.zeros((LANES, d), BF16).at[:heads].set(w_f.T.astype(BF16))
    bf = jnp.zeros((LANES, 1), F32).at[:heads, 0].set(b_f)
    const = lambda shape: pl.BlockSpec(shape, lambda b, i: (0, 0))
    col = lambda rows: pl.BlockSpec((None, rows, tm), lambda b, i: (b, 0, i))
    tshape = lambda rows, dt: jax.ShapeDtypeStruct((batch, rows, seq), dt)
    return pl.pallas_call(
        functools.partial(_fox_proj_kernel, scale=(d // heads) ** -0.5),
        grid=(batch, nl),
        in_specs=[pl.BlockSpec((tm, d), lambda b, i: (b * nl + i, 0)), const((1, d)), const((d, d)),
                  const((d, d)), const((d, d)), const((LANES, d)), const((LANES, 1))],
        out_specs=[pl.BlockSpec((tm, d), lambda b, i: (b * nl + i, 0)),
                   col(d), col(d), col(d), col(d), col(heads)],
        out_shape=[jax.ShapeDtypeStruct((t, d), BF16), tshape(d, F32), tshape(d, BF16),
                   tshape(d, F32), tshape(d, BF16), tshape(heads, F32)],
        compiler_params=_params("parallel", "parallel"),
        name="fox_proj",
    )(x, g.reshape(1, d), wq, wkt, wvt, wft, bf)


def _attn_block(q_ref, kt_ref, vt_ref, ct_ref, o_ref, va_ref, vb_ref, m_ref, acc_ref, *, tq, hd,
                other_work=None):
    pair, i = pl.program_id(1), pl.program_id(2)

    @pl.when(i == 0)
    def _():
        row = lax.broadcasted_iota(jnp.int32, vt_ref.shape, 0)
        vt = vt_ref[...].astype(F32)
        va_ref[...] = jnp.where(row < hd, vt, jnp.where(row == hd, 1.0, 0.0)).astype(BF16)
        vb_ref[...] = jnp.where(row >= hd, vt, jnp.where(row == 0, 1.0, 0.0)).astype(BF16)

    q2 = q_ref[...]
    lo = lax.broadcasted_iota(jnp.int32, (tq, 2 * hd), 1) < hd
    zero = jnp.zeros_like(q2)
    qs = (jnp.where(lo, q2, zero), jnp.where(lo, zero, q2))
    vrefs = (va_ref, vb_ref)
    m_ref[...] = jnp.full_like(m_ref, NEG)
    acc_ref[...] = jnp.zeros_like(acc_ref)

    def step(j, masked):
        sl = pl.ds(pl.multiple_of(j * tq, tq), tq)
        kt = kt_ref[:, sl]
        for h in range(2):
            s = jnp.dot(qs[h], kt, preferred_element_type=F32) - ct_ref[pl.ds(2 * pair + h, 1), sl]
            if masked:
                keep = (lax.broadcasted_iota(jnp.int32, (tq, tq), 1)
                        <= lax.broadcasted_iota(jnp.int32, (tq, tq), 0))
                s = jnp.where(keep, s, NEG)
            m_prev = m_ref[h]
            m_new = jnp.maximum(m_prev, jnp.max(s, axis=1, keepdims=True))
            alpha = jnp.exp(m_prev - m_new)
            p = jnp.exp(s - m_new[:, :1]).astype(BF16)
            acc_ref[h] = alpha * acc_ref[h] + lax.dot_general(p, vrefs[h][:, sl], NT_DIMS,
                                                              preferred_element_type=F32)
            m_ref[h] = m_new

    def full_step(j, carry):
        step(j, False)
        return carry

    lax.fori_loop(0, i, full_step, 0)
    if other_work is not None:
        other_work()
    step(i, True)
    acc_a, acc_b = acc_ref[0], acc_ref[1]
    out = jnp.where(lo, acc_a / acc_a[:, hd:hd + 1], acc_b / acc_b[:, 0:1])
    o_ref[...] = out.astype(o_ref.dtype)


def _fox_attn_kernel(q_ref, kt_ref, vt_ref, ct_ref, o_ref, va_ref, vb_ref, m_ref, acc_ref, *, tq, hd):
    _attn_block(q_ref, kt_ref, vt_ref, ct_ref, o_ref, va_ref, vb_ref, m_ref, acc_ref, tq=tq, hd=hd)


def _fox_attn(q, kt, vt, ct, *, batch, tq):
    t, d = q.shape
    heads = ct.shape[1]
    hd = d // heads
    seq = t // batch
    nq = seq // tq
    pair_cols = pl.BlockSpec((None, 2 * hd, seq), lambda b, p, i: (b, p, 0))
    return pl.pallas_call(
        functools.partial(_fox_attn_kernel, tq=tq, hd=hd),
        grid=(batch, heads // 2, nq),
        in_specs=[
            pl.BlockSpec((tq, 2 * hd), lambda b, p, i: (b * nq + i, p)),
            pair_cols, pair_cols,
            pl.BlockSpec((None, heads, seq), lambda b, p, i: (b, 0, 0)),
        ],
        out_specs=pl.BlockSpec((tq, 2 * hd), lambda b, p, i: (b * nq + i, p)),
        out_shape=jax.ShapeDtypeStruct((t, d), BF16),
        scratch_shapes=[pltpu.VMEM((2 * hd, seq), BF16), pltpu.VMEM((2 * hd, seq), BF16),
                        pltpu.VMEM((2, tq, LANES), F32), pltpu.VMEM((2, tq, 2 * hd), F32)],
        compiler_params=_params("parallel", "parallel", "arbitrary"),
        name="fox_attn",
    )(q, kt, vt, ct)


def _decode_block(step, last, q_ref, kn_ref, vn_ref, fn_ref, k_refs, v_refs, f_refs,
                  o_ref, m_ref, l_ref, acc_ref, cs_ref, *, heads, hd, scale):
    pages = len(k_refs)
    d = heads * hd
    psz = k_refs[0].shape[1]

    @pl.when(step == 0)
    def _():
        m_ref[...] = jnp.full_like(m_ref, NEG)
        l_ref[...] = jnp.zeros_like(l_ref)
        acc_ref[...] = jnp.zeros_like(acc_ref)
        cs_ref[...] = jnp.zeros_like(cs_ref)

    own = (lax.broadcasted_iota(jnp.int32, (heads, d), 1) // hd
           == lax.broadcasted_iota(jnp.int32, (heads, d), 0))
    q_bd = jnp.where(own, q_ref[...] * scale, 0.0)
    q_bd16 = q_bd.astype(BF16)
    upper = (lax.broadcasted_iota(jnp.int32, (psz, psz), 0)
             <= lax.broadcasted_iota(jnp.int32, (psz, psz), 1)).astype(F32)
    csum = _dot_onehot(jnp.concatenate([f[...] for f in f_refs], axis=0), upper)
    base = cs_ref[...]
    logits = []
    for r in range(pages):
        c = csum[r * heads:(r + 1) * heads]
        s = jnp.dot(q_bd16, k_refs[r][...].astype(BF16), preferred_element_type=F32)
        logits.append(s - (c + base))
        base = base + c[:, psz - 1:psz]
    cs_ref[...] = base
    s = jnp.concatenate(logits, axis=1)
    m_prev = m_ref[...]
    m_new = jnp.maximum(m_prev, jnp.max(s, axis=1, keepdims=True))
    alpha = jnp.exp(m_prev - m_new)
    p = jnp.exp(s - m_new)
    l_ref[...] = alpha * l_ref[...] + jnp.sum(p, axis=1, keepdims=True)
    p16 = p.astype(BF16)
    pv = lax.dot_general(p16[:, :psz], v_refs[0][...].astype(BF16), NT_DIMS, preferred_element_type=F32)
    for r in range(1, pages):
        pv = pv + lax.dot_general(p16[:, r * psz:(r + 1) * psz], v_refs[r][...].astype(BF16), NT_DIMS,
                                  preferred_element_type=F32)
    acc_ref[...] = alpha * acc_ref[...] + pv
    m_ref[...] = m_new

    @pl.when(last)
    def _():
        s_new = (jnp.sum(q_bd16.astype(F32) * kn_ref[...], axis=1, keepdims=True)
                 - (cs_ref[...] + fn_ref[...]))
        m_prev = m_ref[...]
        m_new = jnp.maximum(m_prev, s_new)
        alpha = jnp.exp(m_prev - m_new)
        p_new = jnp.exp(s_new - m_new)
        l_fin = alpha * l_ref[...] + p_new
        acc = (alpha * acc_ref[...] + p_new * vn_ref[...]) / l_fin
        o_ref[...] = jnp.sum(jnp.where(own, acc, 0.0), axis=0, keepdims=True)


def _fox_decode_kernel(pt_ref, q_ref, kn_ref, vn_ref, fn_ref, *refs, pages, heads, hd, scale):
    del pt_ref
    k_refs, v_refs, f_refs = refs[:pages], refs[pages:2 * pages], refs[2 * pages:3 * pages]
    step = pl.program_id(1)
    _decode_block(step, step == pl.num_programs(1) - 1, q_ref, kn_ref, vn_ref, fn_ref, k_refs, v_refs,
                  f_refs, *refs[3 * pages:], heads=heads, hd=hd, scale=scale)


def _decode_scratch(heads, d):
    return [pltpu.VMEM((heads, 1), F32), pltpu.VMEM((heads, 1), F32), pltpu.VMEM((heads, d), F32),
            pltpu.VMEM((heads, 1), F32)]


def _fox_decode(q, k_new, v_new, f_new, cache_kt, cache_vt, cache_ft, page_table, layer, *, heads, pages=8):
    rows, _, d = q.shape
    hd = d // heads
    psz = cache_kt.shape[3]
    n_pages = page_table.shape[1]
    pages = min(pages, n_pages)

    def page_map(r):
        return lambda b, s, pt: (layer, pt[b * n_pages + s * pages + r], 0, 0)

    row_spec = pl.BlockSpec((None, 1, d), lambda b, s, pt: (b, 0, 0))
    in_specs = [row_spec, row_spec, row_spec,
                pl.BlockSpec((None, heads, 1), lambda b, s, pt: (b, 0, 0))]
    in_specs += [pl.BlockSpec((None, None, d, psz), page_map(r)) for r in range(pages)]
    in_specs += [pl.BlockSpec((None, None, d, psz), page_map(r)) for r in range(pages)]
    in_specs += [pl.BlockSpec((None, None, heads, psz), page_map(r)) for r in range(pages)]
    return pl.pallas_call(
        functools.partial(_fox_decode_kernel, pages=pages, heads=heads, hd=hd, scale=hd ** -0.5),
        grid_spec=pltpu.PrefetchScalarGridSpec(
            num_scalar_prefetch=1,
            grid=(rows, n_pages // pages),
            in_specs=in_specs,
            out_specs=row_spec,
            scratch_shapes=_decode_scratch(heads, d),
        ),
        out_shape=jax.ShapeDtypeStruct((rows, 1, d), F32),
        compiler_params=_params("parallel", "arbitrary"),
        name="fox_decode",
    )(page_table.reshape(-1), q, k_new, v_new, f_new,
      *([cache_kt] * pages), *([cache_vt] * pages), *([cache_ft] * pages))


def _fox_attn_decode_kernel(pt_ref, q_ref, kt_ref, vt_ref, ct_ref, qd_ref, kn_ref, vn_ref, fn_ref, *refs,
                            pages, tq, heads, hd, scale):
    del pt_ref
    k_refs, v_refs, f_refs = refs[:pages], refs[pages:2 * pages], refs[2 * pages:3 * pages]
    o_ref, od_ref, va_ref, vb_ref, m_ref, acc_ref = refs[3 * pages:3 * pages + 6]
    i = pl.program_id(2)

    def decode():
        _decode_block(i, i == pl.num_programs(2) - 1, qd_ref, kn_ref, vn_ref, fn_ref, k_refs, v_refs,
                      f_refs, od_ref, *refs[3 * pages + 6:], heads=heads, hd=hd, scale=scale)

    _attn_block(q_ref, kt_ref, vt_ref, ct_ref, o_ref, va_ref, vb_ref, m_ref, acc_ref, tq=tq, hd=hd,
                other_work=decode)


def _fox_attn_decode(q, kt, vt, ct, qd, k_new, v_new, f_new, cache_kt, cache_vt, cache_ft, page_table,
                     layer, *, batch, tq):
    t, d = q.shape
    heads = ct.shape[1]
    hd = d // heads
    pairs = heads // 2
    seq = t // batch
    nq = seq // tq
    rows = qd.shape[0]
    psz = cache_kt.shape[3]
    n_pages = page_table.shape[1]
    pages = n_pages // nq
    assert rows == batch * pairs and n_pages == nq * pages

    def page_map(r):
        return lambda b, p, i, pt: (layer, pt[(b * pairs + p) * n_pages + i * pages + r], 0, 0)

    pair_cols = pl.BlockSpec((None, 2 * hd, seq), lambda b, p, i, pt: (b, p, 0))
    row_spec = pl.BlockSpec((None, 1, d), lambda b, p, i, pt: (b * pairs + p, 0, 0))
    in_specs = [pl.BlockSpec((tq, 2 * hd), lambda b, p, i, pt: (b * nq + i, p)), pair_cols, pair_cols,
                pl.BlockSpec((None, heads, seq), lambda b, p, i, pt: (b, 0, 0)),
                row_spec, row_spec, row_spec,
                pl.BlockSpec((None, heads, 1), lambda b, p, i, pt: (b * pairs + p, 0, 0))]
    in_specs += [pl.BlockSpec((None, None, d, psz), page_map(r)) for r in range(pages)]
    in_specs += [pl.BlockSpec((None, None, d, psz), page_map(r)) for r in range(pages)]
    in_specs += [pl.BlockSpec((None, None, heads, psz), page_map(r)) for r in range(pages)]
    return pl.pallas_call(
        functools.partial(_fox_attn_decode_kernel, pages=pages, tq=tq, heads=heads, hd=hd, scale=hd ** -0.5),
        grid_spec=pltpu.PrefetchScalarGridSpec(
            num_scalar_prefetch=1,
            grid=(batch, pairs, nq),
            in_specs=in_specs,
            out_specs=[pl.BlockSpec((tq, 2 * hd), lambda b, p, i, pt: (b * nq + i, p)), row_spec],
            scratch_shapes=[pltpu.VMEM((2 * hd, seq), BF16), pltpu.VMEM((2 * hd, seq), BF16),
                            pltpu.VMEM((2, tq, LANES), F32), pltpu.VMEM((2, tq, 2 * hd), F32)]
            + _decode_scratch(heads, d),
        ),
        out_shape=[jax.ShapeDtypeStruct((t, d), BF16), jax.ShapeDtypeStruct((rows, 1, d), F32)],
        compiler_params=pltpu.CompilerParams(dimension_semantics=("parallel", "parallel", "arbitrary"),
                                             vmem_limit_bytes=FUSED_VMEM_LIMIT),
        name="fox_attn_decode",
    )(page_table.reshape(-1), q, kt, vt, ct, qd, k_new, v_new, f_new,
      *([cache_kt] * pages), *([cache_vt] * pages), *([cache_ft] * pages))


def _conv_kernel(u_ref, w_ref, b_ref, xs_ref, bm_ref, cm_ref, ext_ref, *, taps):
    tl = u_ref.shape[0]
    pad = 8

    @pl.when(pl.program_id(1) == 0)
    def _():
        ext_ref[0:pad, :] = jnp.zeros((pad, ext_ref.shape[1]), F32)

    @pl.when(pl.program_id(1) > 0)
    def _():
        ext_ref[0:pad, :] = ext_ref[tl:tl + pad, :]

    u = u_ref[...]
    ext_ref[pad:pad + tl, :] = u
    y = w_ref[0:1, :] * ext_ref[pl.ds(pad - (taps - 1), tl), :]
    for j in range(1, taps - 1):
        y = y + w_ref[j:j + 1, :] * ext_ref[pl.ds(pad - (taps - 1 - j), tl), :]
    y = y + w_ref[taps - 1:taps, :] * u + b_ref[...]
    y = y * _sigmoid(y)
    di, gn = xs_ref.shape[1], bm_ref.shape[1]
    xs_ref[...] = y[:, :di]
    bm_ref[...] = y[:, di:di + gn]
    cm_ref[...] = y[:, di + gn:]


def _ssd_conv(xbc, w, b, *, batch, di, gn, tl=256):
    t, cdim = xbc.shape
    seq = t // batch
    tl = min(tl, seq)
    nl = seq // tl
    taps = w.shape[0]
    return pl.pallas_call(
        functools.partial(_conv_kernel, taps=taps),
        grid=(batch, nl),
        in_specs=[
            pl.BlockSpec((tl, cdim), lambda bb, l: (bb * nl + l, 0)),
            pl.BlockSpec((taps, cdim), lambda bb, l: (0, 0)),
            pl.BlockSpec((1, cdim), lambda bb, l: (0, 0)),
        ],
        out_specs=[
            pl.BlockSpec((tl, di), lambda bb, l: (bb * nl + l, 0)),
            pl.BlockSpec((tl, gn), lambda bb, l: (bb * nl + l, 0)),
            pl.BlockSpec((tl, gn), lambda bb, l: (bb * nl + l, 0)),
        ],
        out_shape=[jax.ShapeDtypeStruct((t, di), F32), jax.ShapeDtypeStruct((t, gn), F32),
                   jax.ShapeDtypeStruct((t, gn), F32)],
        scratch_shapes=[pltpu.VMEM((tl + 8, cdim), F32)],
        compiler_params=_params("parallel", "arbitrary"),
        name="ssd_conv",
    )(xbc, w, b.reshape(1, cdim))


def _conv_step_kernel(u_ref, st_ref, w_ref, b_ref, o_ref):
    taps = w_ref.shape[0]
    y = w_ref[taps - 1:taps, :] * u_ref[...] + b_ref[...]
    for j in range(taps - 1):
        y = y + w_ref[j:j + 1, :] * st_ref[j]
    o_ref[...] = y * _sigmoid(y)


def _ssd_conv_step(u, state_t, w, b):
    rows, cdim = u.shape
    return pl.pallas_call(
        _conv_step_kernel,
        out_shape=jax.ShapeDtypeStruct((rows, cdim), F32),
        compiler_params=pltpu.CompilerParams(vmem_limit_bytes=VMEM_LIMIT),
        name="ssd_conv_step",
    )(u, state_t, w, b.reshape(1, cdim))


def _ssd_scan_kernel(xs_ref, dt_ref, bm_ref, cm_ref, a_ref, dsk_ref, y_ref, st_ref, ht_ref, *, groups, hd):
    c = pl.program_id(1)
    q, di = xs_ref.shape
    n = bm_ref.shape[1] // groups
    width = di // groups
    hpg = width // hd

    @pl.when(c == 0)
    def _():
        ht_ref[...] = jnp.zeros_like(ht_ref)

    ri = lax.broadcasted_iota(jnp.int32, (q, q), 0)
    ci = lax.broadcasted_iota(jnp.int32, (q, q), 1)
    causal = ci <= ri
    dt = dt_ref[...]
    acum = _onehot_dot(causal.astype(F32), dt * a_ref[...])
    acum_t = acum.T
    lo = lax.broadcasted_iota(jnp.int32, (q, 2 * hd), 1) < hd
    er = lax.broadcasted_iota(jnp.int32, (LANES, width), 0)
    ec = lax.broadcasted_iota(jnp.int32, (LANES, width), 1)
    for grp in range(groups):
        cols = slice(grp * width, (grp + 1) * width)
        ncols = slice(grp * n, (grp + 1) * n)
        expand = (er == grp * hpg + ec // hd).astype(F32)
        acum_x = _dot_onehot(acum, expand)
        dt_x = _dot_onehot(dt, expand)
        xs = xs_ref[:, cols]
        xdt = xs * dt_x
        xdt_b = xdt.astype(BF16)
        bm = bm_ref[:, ncols].astype(BF16)
        cm = cm_ref[:, ncols].astype(BF16)
        cb = lax.dot_general(cm, bm, NT_DIMS, preferred_element_type=F32)
        parts = []
        for pr in range(hpg // 2):
            xp = xdt_b[:, pr * 2 * hd:(pr + 1) * 2 * hd]
            ys = []
            for r in (grp * hpg + 2 * pr, grp * hpg + 2 * pr + 1):
                seg = acum[:, r:r + 1] - acum_t[r:r + 1, :]
                mat = (cb * jnp.exp(jnp.where(causal, seg, NEG))).astype(BF16)
                ys.append(jnp.dot(mat, xp, preferred_element_type=F32))
            parts.append(jnp.where(lo, ys[0], ys[1]))
        y_diag = jnp.concatenate(parts, axis=1)

        ht = ht_ref[:, cols]
        y_off = jnp.dot(cm, ht.astype(BF16), preferred_element_type=F32) * jnp.exp(acum_x)
        y_ref[:, cols] = y_diag + y_off + xs * dsk_ref[:, cols]

        a_last = acum_x[q - 1:q, :]
        xw = (xdt * jnp.exp(a_last - acum_x)).astype(BF16)
        st = jnp.dot(bm_ref[:, ncols].T.astype(BF16), xw, preferred_element_type=F32)
        ht_new = ht * jnp.exp(a_last) + st
        ht_ref[:, cols] = ht_new

        @pl.when(c == pl.num_programs(1) - 1)
        def _():
            st_ref[cols, :] = ht_new.T


def _ssd_scan(xs, dt, bm, cm, a_neg, d_skip, *, batch, groups, hd, chunk=SSD_CHUNK):
    t, di = xs.shape
    gn = bm.shape[1]
    n = gn // groups
    seq = t // batch
    nc = seq // chunk
    heads = di // hd
    a_row = jnp.zeros((1, LANES), F32).at[0, :heads].set(a_neg)
    dsk = jnp.repeat(d_skip, hd).reshape(1, di)
    rows = lambda width: pl.BlockSpec((chunk, width), lambda b, c: (b * nc + c, 0))
    const = lambda width: pl.BlockSpec((1, width), lambda b, c: (0, 0))
    return pl.pallas_call(
        functools.partial(_ssd_scan_kernel, groups=groups, hd=hd),
        grid=(batch, nc),
        in_specs=[rows(di), rows(LANES), rows(gn), rows(gn), const(LANES), const(di)],
        out_specs=[rows(di), pl.BlockSpec((None, di, n), lambda b, c: (b, 0, 0))],
        out_shape=[jax.ShapeDtypeStruct((t, di), F32), jax.ShapeDtypeStruct((batch, di, n), F32)],
        scratch_shapes=[pltpu.VMEM((n, di), F32)],
        compiler_params=_params("parallel", "arbitrary"),
        name="ssd_scan",
    )(xs, dt, bm, cm, a_row, dsk)


def _ssd_step_kernel(h_ref, xs_ref, dt_ref, a_ref, dsk_ref, bm_ref, cm_ref, y_ref, ho_ref, *, groups):
    rows = h_ref.shape[0]
    per = rows // groups
    xc, dtc = xs_ref[...], dt_ref[...]
    decay = jnp.exp(dtc * a_ref[...])
    xdt = xc * dtc
    for g in range(groups):
        sl = slice(g * per, (g + 1) * per)
        hn = h_ref[sl, :] * decay[sl] + xdt[sl] * bm_ref[g:g + 1, :]
        ho_ref[sl, :] = hn
        y_ref[sl, :] = (jnp.sum(hn * cm_ref[g:g + 1, :], axis=1, keepdims=True)
                        + xc[sl] * dsk_ref[sl, :])


def _ssd_step(h0, xs, dt, a_neg, d_skip, bm, cm, *, groups, hd):
    rows, di, n = h0.shape
    col = lambda v: v.reshape(rows, di, 1)
    a_col = jnp.repeat(a_neg, hd).reshape(di, 1)
    dsk = jnp.repeat(d_skip, hd).reshape(di, 1)
    row3 = lambda last: pl.BlockSpec((None, di, last), lambda b: (b, 0, 0))
    grp3 = pl.BlockSpec((None, groups, n), lambda b: (b, 0, 0))
    const = pl.BlockSpec((di, 1), lambda b: (0, 0))
    y, h_new = pl.pallas_call(
        functools.partial(_ssd_step_kernel, groups=groups),
        grid=(rows,),
        in_specs=[row3(n), row3(1), row3(1), const, const, grp3, grp3],
        out_specs=[row3(1), row3(n)],
        out_shape=[jax.ShapeDtypeStruct((rows, di, 1), F32), jax.ShapeDtypeStruct((rows, di, n), F32)],
        compiler_params=_params("parallel"),
        name="ssd_step",
    )(h0, col(xs), col(jnp.repeat(dt, hd, axis=1)), a_col, dsk,
      bm.reshape(rows, groups, n), cm.reshape(rows, groups, n))
    return y.reshape(rows, di), h_new


def _gated_out_kernel(y_ref, z_ref, ng_ref, w_ref, r_ref, o_ref, *, groups):
    z = z_ref[...]
    yg = y_ref[...] * (z * _sigmoid(z))
    per = yg.shape[1] // groups
    parts = []
    for g in range(groups):
        blk = yg[:, g * per:(g + 1) * per]
        parts.append(blk * lax.rsqrt(jnp.mean(blk * blk, axis=-1, keepdims=True) + EPS))
    yn = (jnp.concatenate(parts, axis=1) * ng_ref[...]).astype(BF16)
    o_ref[...] = r_ref[...] + jnp.dot(yn, w_ref[...], preferred_element_type=F32)


def _gated_out(y, z, ng, w, res, *, groups, tm=256):
    m, di = y.shape
    n = w.shape[1]
    tm = min(tm, m)
    return pl.pallas_call(
        functools.partial(_gated_out_kernel, groups=groups),
        grid=(m // tm,),
        in_specs=[
            pl.BlockSpec((tm, di), lambda i: (i, 0)),
            pl.BlockSpec((tm, di), lambda i: (i, 0)),
            pl.BlockSpec((1, di), lambda i: (0, 0)),
            pl.BlockSpec((di, n), lambda i: (0, 0)),
            pl.BlockSpec((tm, n), lambda i: (i, 0)),
        ],
        out_specs=pl.BlockSpec((tm, n), lambda i: (i, 0)),
        out_shape=jax.ShapeDtypeStruct((m, n), F32),
        compiler_params=_params("parallel"),
        name="ssd_gated_out",
    )(y, z, ng.reshape(1, di), w, res)


def _pool_kernel(x_ref, g_ref, w_ref, sc_ref, o_ref, pre_ref, ext_ref):
    tl, d = x_ref.shape
    pad = 16
    l = pl.program_id(1)

    @pl.when(l == 0)
    def _():
        ext_ref[0:pad, :] = jnp.zeros((pad, d), F32)

    @pl.when(l > 0)
    def _():
        ext_ref[0:pad, :] = ext_ref[tl:tl + pad, :]

    x = x_ref[...]
    xn = _rms(x, g_ref[...])
    ext_ref[pad:pad + tl, :] = xn
    pos = l * tl + lax.broadcasted_iota(jnp.int32, (tl, 1), 0)
    gd = d // len(POOL_WINDOWS)
    for gi, win in enumerate(POOL_WINDOWS):
        cols = slice(gi * gd, (gi + 1) * gd)
        tot = xn[:, cols]
        for j in range(1, win):
            tot = tot + ext_ref[pl.ds(pad - j, tl), cols]
        cnt = jnp.minimum(pos + 1, win).astype(F32)
        pooled = tot / cnt - xn[:, cols]
        mixed = jnp.dot(pooled.astype(BF16), w_ref[gi], preferred_element_type=F32)
        o_ref[:, cols] = x[:, cols] + mixed * sc_ref[:, cols]

    @pl.when(l == pl.num_programs(1) - 1)
    def _():
        pre_ref[...] = xn[tl - pad:, :]


def _pool(x, g, w_grp, scale, *, batch, tl=512):
    t, d = x.shape
    seq = t // batch
    tl = min(tl, seq)
    nl = seq // tl
    ng, gd = w_grp.shape[0], w_grp.shape[1]
    return pl.pallas_call(
        _pool_kernel,
        grid=(batch, nl),
        in_specs=[
            pl.BlockSpec((tl, d), lambda b, l: (b * nl + l, 0)),
            pl.BlockSpec((1, d), lambda b, l: (0, 0)),
            pl.BlockSpec((ng, gd, gd), lambda b, l: (0, 0, 0)),
            pl.BlockSpec((1, d), lambda b, l: (0, 0)),
        ],
        out_specs=[
            pl.BlockSpec((tl, d), lambda b, l: (b * nl + l, 0)),
            pl.BlockSpec((None, 16, d), lambda b, l: (b, 0, 0)),
        ],
        out_shape=[jax.ShapeDtypeStruct((t, d), F32), jax.ShapeDtypeStruct((batch, 16, d), F32)],
        scratch_shapes=[pltpu.VMEM((tl + 16, d), F32)],
        compiler_params=_params("parallel", "arbitrary"),
        name="pool",
    )(x, g.reshape(1, d), w_grp, scale.reshape(1, d))


def _pool_step_kernel(x_ref, st_ref, g_ref, w_ref, sc_ref, o_ref, xn_ref, *, pos):
    x = x_ref[...]
    xn = _rms(x, g_ref[...])
    xn_ref[...] = xn
    past = st_ref.shape[0]
    gd = x.shape[1] // len(POOL_WINDOWS)
    for gi, win in enumerate(POOL_WINDOWS):
        cols = slice(gi * gd, (gi + 1) * gd)
        tot = xn[:, cols]
        for j in range(1, win):
            tot = tot + st_ref[past - j][:, cols]
        pooled = tot / float(min(pos + 1, win)) - xn[:, cols]
        mixed = jnp.dot(pooled.astype(BF16), w_ref[gi], preferred_element_type=F32)
        o_ref[:, cols] = x[:, cols] + mixed * sc_ref[:, cols]


def _pool_step(x, state_t, g, w_grp, scale, *, pos):
    rows, d = x.shape
    return pl.pallas_call(
        functools.partial(_pool_step_kernel, pos=pos),
        out_shape=[jax.ShapeDtypeStruct((rows, d), F32), jax.ShapeDtypeStruct((rows, d), F32)],
        compiler_params=pltpu.CompilerParams(vmem_limit_bytes=VMEM_LIMIT),
        name="pool_step",
    )(x, state_t, g.reshape(1, d), w_grp, scale.reshape(1, d))


def _fox_proj_rows(h, g, w_qkv, w_f, b_f):
    d = h.shape[1]
    heads = w_f.shape[1]
    (qkv,) = _norm_linear(h, g, w_qkv.astype(BF16))
    wf = jnp.zeros((d, LANES), BF16).at[:, :heads].set(w_f.astype(BF16))
    bf = jnp.zeros((LANES,), F32).at[:heads].set(b_f)
    (logf,) = _norm_linear(h, g, wf, bf, act="log_sigmoid")
    return qkv[:, :d], qkv[:, d:2 * d], qkv[:, 2 * d:], logf[:, :heads]


def kernel(x_prompt, x_sample, cache_k, cache_v, cache_logf, page_table, state_ssm, state_conv, state_pool, norm_mix, norm_mlp, final_norm, fox_w_qkv, fox_w_f, fox_b_f, fox_w_o, ssd_w_in, ssd_conv_w, ssd_conv_b, ssd_dt_bias, ssd_a_log, ssd_d, ssd_norm, ssd_w_out, pool_w, pool_scale, mlp_w_up, mlp_w_down):
    bp, lp, d = x_prompt.shape
    bs, ls, _ = x_sample.shape
    assert ls == 1, "the sample group decodes one token per row"
    depth = norm_mix.shape[0]
    heads = fox_w_f.shape[-1]
    hd = d // heads
    n_layers, n_pool, psz = cache_k.shape[:3]
    past = page_table.shape[1] * psz
    ssd_heads = ssd_a_log.shape[1]
    di = ssd_norm.shape[1]
    cdim = ssd_conv_b.shape[1]
    n_state = state_ssm.shape[-1]
    ssd_hd = state_ssm.shape[-2]
    groups = (cdim - di) // (2 * n_state)
    gn = groups * n_state
    pool_past = state_pool.shape[2]

    hp = x_prompt.reshape(bp * lp, d)
    hs = x_sample.reshape(bs, d)
    cache_kt = jnp.transpose(cache_k, (0, 1, 3, 4, 2)).reshape(n_layers, n_pool, d, psz)
    cache_vt = jnp.transpose(cache_v, (0, 1, 3, 4, 2)).reshape(n_layers, n_pool, d, psz)
    cache_ft = jnp.swapaxes(cache_logf, 2, 3)
    outs = {name: [] for name in ("kp", "vp", "fp", "ks", "vs", "fs", "ssm_p", "conv_p", "ssm_s",
                                  "conv_s", "pool_p", "pool_s")}
    i_fox = i_ssd = i_pool = 0
    for layer in range(depth):
        kind = layer % 3
        g = norm_mix[layer]
        if kind == 0:
            j = i_fox
            w_o = fox_w_o[j].astype(BF16)
            q, kt, kt16, vt, vt16, ft = _fox_proj(hp, g, fox_w_qkv[j], fox_w_f[j], fox_b_f[j], batch=bp)
            ct = _cumsum_lanes(ft)
            qd, k, v, logf = _fox_proj_rows(hs, g, fox_w_qkv[j], fox_w_f[j], fox_b_f[j])
            dec_args = (qd.reshape(bs, 1, d), k.reshape(bs, 1, d), v.reshape(bs, 1, d),
                        logf.reshape(bs, heads, 1), cache_kt, cache_vt, cache_ft, page_table, j)
            tq = min(ATTN_BLOCK, lp)
            if bs == bp * (heads // 2) and page_table.shape[1] % (lp // tq) == 0:
                ctx, ctx_s = _fox_attn_decode(q, kt16, vt16, ct, *dec_args, batch=bp, tq=tq)
            else:
                ctx = _fox_attn(q, kt16, vt16, ct, batch=bp, tq=tq)
                ctx_s = _fox_decode(*dec_args, heads=heads)
            hp = _linear_res(ctx, w_o, hp)
            outs["kp"].append(jnp.transpose(kt.reshape(bp, heads, hd, lp), (0, 3, 1, 2)))
            outs["vp"].append(jnp.transpose(vt.reshape(bp, heads, hd, lp), (0, 3, 1, 2)))
            outs["fp"].append(jnp.swapaxes(ft, 1, 2))
            hs = _linear_res(ctx_s.reshape(bs, d).astype(BF16), w_o, hs)
            outs["ks"].append(k.reshape(bs, 1, heads, hd))
            outs["vs"].append(v.reshape(bs, 1, heads, hd))
            outs["fs"].append(logf.reshape(bs, 1, heads))
            i_fox += 1
        elif kind == 1:
            j = i_ssd
            w_in = ssd_w_in[j]
            w_z = w_in[:, :di].astype(BF16)
            w_xbc = w_in[:, di:di + cdim].astype(BF16)
            w_dt = jnp.zeros((d, LANES), BF16).at[:, :ssd_heads].set(w_in[:, di + cdim:].astype(BF16))
            b_dt = jnp.zeros((LANES,), F32).at[:ssd_heads].set(ssd_dt_bias[j])
            a_neg = -jnp.exp(ssd_a_log[j])
            w_out = ssd_w_out[j].astype(BF16)

            def in_proj(h):
                (z,) = _norm_linear(h, g, w_z)
                (xbc,) = _norm_linear(h, g, w_xbc)
                (dt,) = _norm_linear(h, g, w_dt, b_dt, act="softplus")
                return z, xbc, dt

            z, xbc, dt = in_proj(hp)
            xs, bm, cm = _ssd_conv(xbc, ssd_conv_w[j], ssd_conv_b[j], batch=bp, di=di, gn=gn)
            y, h_fin = _ssd_scan(xs, dt, bm, cm, a_neg, ssd_d[j], batch=bp, groups=groups, hd=ssd_hd)
            hp = _gated_out(y, z, ssd_norm[j], w_out, hp, groups=groups)
            outs["ssm_p"].append(h_fin.reshape(bp, ssd_heads, ssd_hd, n_state))
            outs["conv_p"].append(xbc.reshape(bp, lp, cdim)[:, lp - (ssd_conv_w.shape[1] - 1):])

            z, xbc, dt = in_proj(hs)
            conv_state = state_conv[j]
            xc = _ssd_conv_step(xbc, jnp.swapaxes(conv_state, 0, 1), ssd_conv_w[j], ssd_conv_b[j])
            y, h_new = _ssd_step(state_ssm[j].reshape(bs, di, n_state), xc[:, :di], dt[:, :ssd_heads],
                                 a_neg, ssd_d[j], xc[:, di:di + gn], xc[:, di + gn:],
                                 groups=groups, hd=ssd_hd)
            hs = _gated_out(y, z, ssd_norm[j], w_out, hs, groups=groups)
            outs["ssm_s"].append(h_new.reshape(bs, ssd_heads, ssd_hd, n_state))
            outs["conv_s"].append(jnp.concatenate([conv_state[:, 1:], xbc[:, None]], axis=1))
            i_ssd += 1
        else:
            j = i_pool
            w_grp = pool_w[j].astype(BF16)
            hp, pre = _pool(hp, g, w_grp, pool_scale[j], batch=bp)
            outs["pool_p"].append(pre[:, 16 - pool_past:])
            pool_state = state_pool[j]
            hs, xn_s = _pool_step(hs, jnp.swapaxes(pool_state, 0, 1), g, w_grp, pool_scale[j], pos=past)
            outs["pool_s"].append(jnp.concatenate([pool_state[:, 1:], xn_s[:, None]], axis=1))
            i_pool += 1
        w_up, w_down = mlp_w_up[layer].astype(BF16), mlp_w_down[layer].astype(BF16)
        final_g = final_norm if layer == depth - 1 else None
        hp = _mlp(hp, norm_mlp[layer], w_up, w_down, final_g)
        hs = _mlp(hs, norm_mlp[layer], w_up, w_down, final_g)
    y_prompt = hp.reshape(bp, lp, d)
    y_sample = hs.reshape(bs, ls, d)
    return (y_prompt, y_sample) + tuple(jnp.stack(outs[name]) for name in (
        "kp", "vp", "fp", "ks", "vs", "fs", "ssm_p", "conv_p", "ssm_s", "conv_s", "pool_p", "pool_s"))
```

```python
import functools

import jax
import jax.numpy as jnp
from jax import lax
from jax.experimental import pallas as pl
from jax.experimental.pallas import tpu as pltpu

F32 = jnp.float32
BF16 = jnp.bfloat16
EPS = 1e-6
NEG = -1e30
LANES = 128
SSD_CHUNK = 128
ATTN_BLOCK = 1024
POOL_WINDOWS = (2, 4, 8, 16)
VMEM_LIMIT = 48 * 1024 * 1024
FUSED_VMEM_LIMIT = 58 * 1024 * 1024
NT_DIMS = (((1,), (1,)), ((), ()))


def _params(*sem):
    return pltpu.CompilerParams(dimension_semantics=sem, vmem_limit_bytes=VMEM_LIMIT)


def _rms(x, g):
    return x * lax.rsqrt(jnp.mean(x * x, axis=-1, keepdims=True) + EPS) * g


def _softplus(x):
    return jnp.maximum(x, 0.0) + jnp.log1p(jnp.exp(-jnp.abs(x)))


def _sigmoid(x):
    return 1.0 / (1.0 + jnp.exp(-x))


def _split3(x):
    hi = x.astype(BF16)
    rest = x - hi.astype(F32)
    mid = rest.astype(BF16)
    return hi, mid, (rest - mid.astype(F32)).astype(BF16)


def _dot_onehot(x, sel):
    sel = sel.astype(BF16)
    return sum(jnp.dot(piece, sel, preferred_element_type=F32) for piece in _split3(x))


def _onehot_dot(sel, x):
    sel = sel.astype(BF16)
    return sum(jnp.dot(sel, piece, preferred_element_type=F32) for piece in _split3(x))


def _norm_linear_kernel(x_ref, g_ref, w_ref, b_ref, *refs, act, outs):
    out_refs, xn_ref = refs[:-1], refs[-1]

    @pl.when(pl.program_id(1) == 0)
    def _():
        xn_ref[...] = _rms(x_ref[...], g_ref[...]).astype(BF16)

    y = jnp.dot(xn_ref[...], w_ref[...], preferred_element_type=F32) + b_ref[...]
    if act == "log_sigmoid":
        y = -_softplus(-y)
    elif act == "softplus":
        y = _softplus(y)
    for o_ref, (dtype, scale) in zip(out_refs, outs):
        o_ref[...] = (y * scale).astype(dtype)


def _norm_linear(x, g, w, b=None, *, act=None, outs=((F32, 1.0),), tm=512, tn=1024):
    m, k = x.shape
    n = w.shape[1]
    tm, tn = min(tm, m), min(tn, n)
    if b is None:
        b = jnp.zeros((n,), F32)
    res = pl.pallas_call(
        functools.partial(_norm_linear_kernel, act=act, outs=outs),
        grid=(m // tm, n // tn),
        in_specs=[
            pl.BlockSpec((tm, k), lambda i, j: (i, 0)),
            pl.BlockSpec((1, k), lambda i, j: (0, 0)),
            pl.BlockSpec((k, tn), lambda i, j: (0, j)),
            pl.BlockSpec((1, tn), lambda i, j: (0, j)),
        ],
        out_specs=[pl.BlockSpec((tm, tn), lambda i, j: (i, j)) for _ in outs],
        out_shape=[jax.ShapeDtypeStruct((m, n), d) for d, _ in outs],
        scratch_shapes=[pltpu.VMEM((tm, k), BF16)],
        compiler_params=_params("parallel", "arbitrary"),
        name="norm_linear",
    )(x, g.reshape(1, k), w, b.reshape(1, n))
    return res


def _linear_res_kernel(a_ref, w_ref, r_ref, o_ref):
    o_ref[...] = r_ref[...] + jnp.dot(a_ref[...], w_ref[...], preferred_element_type=F32)


def _linear_res(a, w, res, *, tm=512):
    m, k = a.shape
    n = w.shape[1]
    tm = min(tm, m)
    return pl.pallas_call(
        _linear_res_kernel,
        grid=(m // tm,),
        in_specs=[
            pl.BlockSpec((tm, k), lambda i: (i, 0)),
            pl.BlockSpec((k, n), lambda i: (0, 0)),
            pl.BlockSpec((tm, n), lambda i: (i, 0)),
        ],
        out_specs=pl.BlockSpec((tm, n), lambda i: (i, 0)),
        out_shape=jax.ShapeDtypeStruct((m, n), F32),
        compiler_params=_params("parallel"),
        name="linear_res",
    )(a, w, res)


def _mlp_kernel(x_ref, g_ref, wu_ref, wd_ref, *refs, final_norm):
    o_ref, xn_ref, acc_ref = refs[-3:]
    f = pl.program_id(1)

    @pl.when(f == 0)
    def _():
        xn_ref[...] = _rms(x_ref[...], g_ref[...]).astype(BF16)
        acc_ref[...] = jnp.zeros_like(acc_ref)

    h = jnp.maximum(jnp.dot(xn_ref[...], wu_ref[...], preferred_element_type=F32), 0.0)
    acc_ref[...] += jnp.dot((h * h).astype(BF16), wd_ref[...], preferred_element_type=F32)

    @pl.when(f == pl.num_programs(1) - 1)
    def _():
        y = x_ref[...] + acc_ref[...]
        o_ref[...] = _rms(y, refs[0][...]) if final_norm else y


def _mlp(x, g, w_up, w_down, final_g=None, *, tm=1024, tf=1024):
    m, d = x.shape
    ff = w_up.shape[1]
    tm = min(tm, m)
    gain = pl.BlockSpec((1, d), lambda i, f: (0, 0))
    final = [] if final_g is None else [final_g.reshape(1, d)]
    return pl.pallas_call(
        functools.partial(_mlp_kernel, final_norm=final_g is not None),
        grid=(m // tm, ff // tf),
        in_specs=[
            pl.BlockSpec((tm, d), lambda i, f: (i, 0)),
            gain,
            pl.BlockSpec((d, tf), lambda i, f: (0, f)),
            pl.BlockSpec((tf, d), lambda i, f: (f, 0)),
        ] + [gain] * len(final),
        out_specs=pl.BlockSpec((tm, d), lambda i, f: (i, 0)),
        out_shape=jax.ShapeDtypeStruct((m, d), F32),
        scratch_shapes=[pltpu.VMEM((tm, d), BF16), pltpu.VMEM((tm, d), F32)],
        compiler_params=_params("parallel", "arbitrary"),
        name="mlp",
    )(x, g.reshape(1, d), w_up, w_down, *final)


def _cumsum_kernel(x_ref, o_ref):
    rows, length = x_ref.shape
    r = lax.broadcasted_iota(jnp.int32, (LANES, LANES), 0)
    c = lax.broadcasted_iota(jnp.int32, (LANES, LANES), 1)
    upper = (r <= c).astype(F32)

    carry = jnp.zeros((rows, 1), F32)
    for j in range(length // LANES):
        sl = slice(j * LANES, (j + 1) * LANES)
        cs = _dot_onehot(x_ref[:, sl], upper) + carry
        o_ref[:, sl] = cs
        carry = cs[:, LANES - 1:LANES]


def _cumsum_lanes(x):
    b, rows, length = x.shape
    return pl.pallas_call(
        _cumsum_kernel,
        grid=(b,),
        in_specs=[pl.BlockSpec((None, rows, length), lambda i: (i, 0, 0))],
        out_specs=pl.BlockSpec((None, rows, length), lambda i: (i, 0, 0)),
        out_shape=jax.ShapeDtypeStruct(x.shape, F32),
        compiler_params=_params("parallel"),
        name="cumsum_lanes",
    )(x)


def _fox_proj_kernel(x_ref, g_ref, wq_ref, wkt_ref, wvt_ref, wft_ref, bf_ref, *refs, scale, layer, first):
    q_ref, kt_ref, kt16_ref, vt_ref, vt16_ref, ft_ref = refs[-6:]
    xn = _rms(x_ref[...], g_ref[...]).astype(BF16)
    q_ref[...] = (jnp.dot(xn, wq_ref[...], preferred_element_type=F32) * scale).astype(q_ref.dtype)
    for w_ref, o_ref, o16_ref in ((wkt_ref, kt_ref, kt16_ref), (wvt_ref, vt_ref, vt16_ref)):
        yt = lax.dot_general(w_ref[...], xn, NT_DIMS, preferred_element_type=F32)
        o16_ref[...] = yt.astype(BF16)
        if first:
            for slot in range(o_ref.shape[0]):
                o_ref[slot] = yt if slot == layer else jnp.zeros_like(yt)
        else:
            o_ref[...] = yt
    ft = lax.dot_general(wft_ref[...], xn, NT_DIMS, preferred_element_type=F32) + bf_ref[...]
    ft_ref[...] = -_softplus(-ft[:ft_ref.shape[0]])


def _fox_proj(x, g, w_qkv, w_f, b_f, *, batch, layer, n_layers, stacked=None, tm=512):
    t, d = x.shape
    heads = w_f.shape[1]
    seq = t // batch
    tm = min(tm, seq)
    nl = seq // tm
    wq = w_qkv[:, :d].astype(BF16)
    wkt = w_qkv[:, d:2 * d].T.astype(BF16)
    wvt = w_qkv[:, 2 * d:].T.astype(BF16)
    wft = jnp.zeros((LANES, d), BF16).at[:heads].set(w_f.T.astype(BF16))
    bf = jnp.zeros((LANES, 1), F32).at[:heads, 0].set(b_f)
    const = lambda shape: pl.BlockSpec(shape, lambda b, i: (0, 0))
    col = lambda rows: pl.BlockSpec((None, rows, tm), lambda b, i: (b, 0, i))
    tshape = lambda rows, dt: jax.ShapeDtypeStruct((batch, rows, seq), dt)
    first = stacked is None
    if first:
        slot_spec = pl.BlockSpec((n_layers, None, d, tm), lambda b, i: (0, b, 0, i))
        extra, extra_specs, aliases = (), [], {}
    else:
        slot_spec = pl.BlockSpec((None, None, d, tm), lambda b, i: (layer, b, 0, i))
        extra, extra_specs = tuple(stacked), [pl.BlockSpec(memory_space=pl.ANY)] * 2
        aliases = {7: 1, 8: 3}
    stack_shape = jax.ShapeDtypeStruct((n_layers, batch, d, seq), F32)
    return pl.pallas_call(
        functools.partial(_fox_proj_kernel, scale=(d // heads) ** -0.5, layer=layer, first=first),
        grid=(batch, nl),
        in_specs=[pl.BlockSpec((tm, d), lambda b, i: (b * nl + i, 0)), const((1, d)), const((d, d)),
                  const((d, d)), const((d, d)), const((LANES, d)), const((LANES, 1))] + extra_specs,
        out_specs=[pl.BlockSpec((tm, d), lambda b, i: (b * nl + i, 0)),
                   slot_spec, col(d), slot_spec, col(d), col(heads)],
        out_shape=[jax.ShapeDtypeStruct((t, d), BF16), stack_shape, tshape(d, BF16),
                   stack_shape, tshape(d, BF16), tshape(heads, F32)],
        input_output_aliases=aliases,
        compiler_params=_params("parallel", "parallel"),
        name="fox_proj",
    )(x, g.reshape(1, d), wq, wkt, wvt, wft, bf, *extra)


def _attn_block(q_ref, kt_ref, vt_ref, ct_ref, o_ref, va_ref, vb_ref, m_ref, acc_ref, *, tq, hd,
                other_work=None):
    pair, i = pl.program_id(1), pl.program_id(2)

    @pl.when(i == 0)
    def _():
        row = lax.broadcasted_iota(jnp.int32, vt_ref.shape, 0)
        vt = vt_ref[...].astype(F32)
        va_ref[...] = jnp.where(row < hd, vt, jnp.where(row == hd, 1.0, 0.0)).astype(BF16)
        vb_ref[...] = jnp.where(row >= hd, vt, jnp.where(row == 0, 1.0, 0.0)).astype(BF16)

    q2 = q_ref[...]
    lo = lax.broadcasted_iota(jnp.int32, (tq, 2 * hd), 1) < hd
    zero = jnp.zeros_like(q2)
    qs = (jnp.where(lo, q2, zero), jnp.where(lo, zero, q2))
    vrefs = (va_ref, vb_ref)
    m_ref[...] = jnp.full_like(m_ref, NEG)
    acc_ref[...] = jnp.zeros_like(acc_ref)

    def step(j, masked):
        sl = pl.ds(pl.multiple_of(j * tq, tq), tq)
        kt = kt_ref[:, sl]
        for h in range(2):
            s = jnp.dot(qs[h], kt, preferred_element_type=F32) - ct_ref[pl.ds(2 * pair + h, 1), sl]
            if masked:
                keep = (lax.broadcasted_iota(jnp.int32, (tq, tq), 1)
                        <= lax.broadcasted_iota(jnp.int32, (tq, tq), 0))
                s = jnp.where(keep, s, NEG)
            m_prev = m_ref[h]
            m_new = jnp.maximum(m_prev, jnp.max(s, axis=1, keepdims=True))
            alpha = jnp.exp(m_prev - m_new)
            p = jnp.exp(s - m_new[:, :1]).astype(BF16)
            acc_ref[h] = alpha * acc_ref[h] + lax.dot_general(p, vrefs[h][:, sl], NT_DIMS,
                                                              preferred_element_type=F32)
            m_ref[h] = m_new

    def full_step(j, carry):
        step(j, False)
        return carry

    lax.fori_loop(0, i, full_step, 0)
    if other_work is not None:
        other_work()
    step(i, True)
    acc_a, acc_b = acc_ref[0], acc_ref[1]
    out = jnp.where(lo, acc_a / acc_a[:, hd:hd + 1], acc_b / acc_b[:, 0:1])
    o_ref[...] = out.astype(o_ref.dtype)


def _fox_attn_kernel(q_ref, kt_ref, vt_ref, ct_ref, o_ref, va_ref, vb_ref, m_ref, acc_ref, *, tq, hd):
    _attn_block(q_ref, kt_ref, vt_ref, ct_ref, o_ref, va_ref, vb_ref, m_ref, acc_ref, tq=tq, hd=hd)


def _fox_attn(q, kt, vt, ct, *, batch, tq):
    t, d = q.shape
    heads = ct.shape[1]
    hd = d // heads
    seq = t // batch
    nq = seq // tq
    pair_cols = pl.BlockSpec((None, 2 * hd, seq), lambda b, p, i: (b, p, 0))
    return pl.pallas_call(
        functools.partial(_fox_attn_kernel, tq=tq, hd=hd),
        grid=(batch, heads // 2, nq),
        in_specs=[
            pl.BlockSpec((tq, 2 * hd), lambda b, p, i: (b * nq + i, p)),
            pair_cols, pair_cols,
            pl.BlockSpec((None, heads, seq), lambda b, p, i: (b, 0, 0)),
        ],
        out_specs=pl.BlockSpec((tq, 2 * hd), lambda b, p, i: (b * nq + i, p)),
        out_shape=jax.ShapeDtypeStruct((t, d), BF16),
        scratch_shapes=[pltpu.VMEM((2 * hd, seq), BF16), pltpu.VMEM((2 * hd, seq), BF16),
                        pltpu.VMEM((2, tq, LANES), F32), pltpu.VMEM((2, tq, 2 * hd), F32)],
        compiler_params=_params("parallel", "parallel", "arbitrary"),
        name="fox_attn",
    )(q, kt, vt, ct)


def _decode_block(step, last, q_ref, kn_ref, vn_ref, fn_ref, k_refs, v_refs, f_refs,
                  o_ref, m_ref, l_ref, acc_ref, cs_ref, *, heads, hd, scale):
    pages = len(k_refs)
    d = heads * hd
    psz = k_refs[0].shape[1]

    @pl.when(step == 0)
    def _():
        m_ref[...] = jnp.full_like(m_ref, NEG)
        l_ref[...] = jnp.zeros_like(l_ref)
        acc_ref[...] = jnp.zeros_like(acc_ref)
        cs_ref[...] = jnp.zeros_like(cs_ref)

    own = (lax.broadcasted_iota(jnp.int32, (heads, d), 1) // hd
           == lax.broadcasted_iota(jnp.int32, (heads, d), 0))
    q_bd = jnp.where(own, q_ref[...] * scale, 0.0)
    q_bd16 = q_bd.astype(BF16)
    upper = (lax.broadcasted_iota(jnp.int32, (psz, psz), 0)
             <= lax.broadcasted_iota(jnp.int32, (psz, psz), 1)).astype(F32)
    csum = _dot_onehot(jnp.concatenate([f[...] for f in f_refs], axis=0), upper)
    base = cs_ref[...]
    logits = []
    for r in range(pages):
        c = csum[r * heads:(r + 1) * heads]
        s = jnp.dot(q_bd16, k_refs[r][...].astype(BF16), preferred_element_type=F32)
        logits.append(s - (c + base))
        base = base + c[:, psz - 1:psz]
    cs_ref[...] = base
    s = jnp.concatenate(logits, axis=1)
    m_prev = m_ref[...]
    m_new = jnp.maximum(m_prev, jnp.max(s, axis=1, keepdims=True))
    alpha = jnp.exp(m_prev - m_new)
    p = jnp.exp(s - m_new)
    l_ref[...] = alpha * l_ref[...] + jnp.sum(p, axis=1, keepdims=True)
    p16 = p.astype(BF16)
    pv = lax.dot_general(p16[:, :psz], v_refs[0][...].astype(BF16), NT_DIMS, preferred_element_type=F32)
    for r in range(1, pages):
        pv = pv + lax.dot_general(p16[:, r * psz:(r + 1) * psz], v_refs[r][...].astype(BF16), NT_DIMS,
                                  preferred_element_type=F32)
    acc_ref[...] = alpha * acc_ref[...] + pv
    m_ref[...] = m_new

    @pl.when(last)
    def _():
        s_new = (jnp.sum(q_bd16.astype(F32) * kn_ref[...], axis=1, keepdims=True)
                 - (cs_ref[...] + fn_ref[...]))
        m_prev = m_ref[...]
        m_new = jnp.maximum(m_prev, s_new)
        alpha = jnp.exp(m_prev - m_new)
        p_new = jnp.exp(s_new - m_new)
        l_fin = alpha * l_ref[...] + p_new
        acc = (alpha * acc_ref[...] + p_new * vn_ref[...]) / l_fin
        o_ref[...] = jnp.sum(jnp.where(own, acc, 0.0), axis=0, keepdims=True)


def _fox_decode_kernel(pt_ref, q_ref, kn_ref, vn_ref, fn_ref, *refs, pages, heads, hd, scale):
    del pt_ref
    k_refs, v_refs, f_refs = refs[:pages], refs[pages:2 * pages], refs[2 * pages:3 * pages]
    step = pl.program_id(1)
    _decode_block(step, step == pl.num_programs(1) - 1, q_ref, kn_ref, vn_ref, fn_ref, k_refs, v_refs,
                  f_refs, *refs[3 * pages:], heads=heads, hd=hd, scale=scale)


def _decode_scratch(heads, d):
    return [pltpu.VMEM((heads, 1), F32), pltpu.VMEM((heads, 1), F32), pltpu.VMEM((heads, d), F32),
            pltpu.VMEM((heads, 1), F32)]


def _fox_decode(q, k_new, v_new, f_new, cache_kt, cache_vt, cache_ft, page_table, layer, *, heads, pages=8):
    rows, _, d = q.shape
    hd = d // heads
    psz = cache_kt.shape[3]
    n_pages = page_table.shape[1]
    pages = min(pages, n_pages)

    def page_map(r):
        return lambda b, s, pt: (layer, pt[b * n_pages + s * pages + r], 0, 0)

    row_spec = pl.BlockSpec((None, 1, d), lambda b, s, pt: (b, 0, 0))
    in_specs = [row_spec, row_spec, row_spec,
                pl.BlockSpec((None, heads, 1), lambda b, s, pt: (b, 0, 0))]
    in_specs += [pl.BlockSpec((None, None, d, psz), page_map(r)) for r in range(pages)]
    in_specs += [pl.BlockSpec((None, None, d, psz), page_map(r)) for r in range(pages)]
    in_specs += [pl.BlockSpec((None, None, heads, psz), page_map(r)) for r in range(pages)]
    return pl.pallas_call(
        functools.partial(_fox_decode_kernel, pages=pages, heads=heads, hd=hd, scale=hd ** -0.5),
        grid_spec=pltpu.PrefetchScalarGridSpec(
            num_scalar_prefetch=1,
            grid=(rows, n_pages // pages),
            in_specs=in_specs,
            out_specs=row_spec,
            scratch_shapes=_decode_scratch(heads, d),
        ),
        out_shape=jax.ShapeDtypeStruct((rows, 1, d), F32),
        compiler_params=_params("parallel", "arbitrary"),
        name="fox_decode",
    )(page_table.reshape(-1), q, k_new, v_new, f_new,
      *([cache_kt] * pages), *([cache_vt] * pages), *([cache_ft] * pages))


def _fox_attn_decode_kernel(pt_ref, q_ref, kt_ref, vt_ref, ct_ref, qd_ref, kn_ref, vn_ref, fn_ref, *refs,
                            pages, tq, heads, hd, scale):
    del pt_ref
    k_refs, v_refs, f_refs = refs[:pages], refs[pages:2 * pages], refs[2 * pages:3 * pages]
    o_ref, od_ref, va_ref, vb_ref, m_ref, acc_ref = refs[3 * pages:3 * pages + 6]
    i = pl.program_id(2)

    def decode():
        _decode_block(i, i == pl.num_programs(2) - 1, qd_ref, kn_ref, vn_ref, fn_ref, k_refs, v_refs,
                      f_refs, od_ref, *refs[3 * pages + 6:], heads=heads, hd=hd, scale=scale)

    _attn_block(q_ref, kt_ref, vt_ref, ct_ref, o_ref, va_ref, vb_ref, m_ref, acc_ref, tq=tq, hd=hd,
                other_work=decode)


def _fox_attn_decode(q, kt, vt, ct, qd, k_new, v_new, f_new, cache_kt, cache_vt, cache_ft, page_table,
                     layer, *, batch, tq):
    t, d = q.shape
    heads = ct.shape[1]
    hd = d // heads
    pairs = heads // 2
    seq = t // batch
    nq = seq // tq
    rows = qd.shape[0]
    psz = cache_kt.shape[3]
    n_pages = page_table.shape[1]
    pages = n_pages // nq
    assert rows == batch * pairs and n_pages == nq * pages

    def page_map(r):
        return lambda b, p, i, pt: (layer, pt[(b * pairs + p) * n_pages + i * pages + r], 0, 0)

    pair_cols = pl.BlockSpec((None, 2 * hd, seq), lambda b, p, i, pt: (b, p, 0))
    row_spec = pl.BlockSpec((None, 1, d), lambda b, p, i, pt: (b * pairs + p, 0, 0))
    in_specs = [pl.BlockSpec((tq, 2 * hd), lambda b, p, i, pt: (b * nq + i, p)), pair_cols, pair_cols,
                pl.BlockSpec((None, heads, seq), lambda b, p, i, pt: (b, 0, 0)),
                row_spec, row_spec, row_spec,
                pl.BlockSpec((None, heads, 1), lambda b, p, i, pt: (b * pairs + p, 0, 0))]
    in_specs += [pl.BlockSpec((None, None, d, psz), page_map(r)) for r in range(pages)]
    in_specs += [pl.BlockSpec((None, None, d, psz), page_map(r)) for r in range(pages)]
    in_specs += [pl.BlockSpec((None, None, heads, psz), page_map(r)) for r in range(pages)]
    return pl.pallas_call(
        functools.partial(_fox_attn_decode_kernel, pages=pages, tq=tq, heads=heads, hd=hd, scale=hd ** -0.5),
        grid_spec=pltpu.PrefetchScalarGridSpec(
            num_scalar_prefetch=1,
            grid=(batch, pairs, nq),
            in_specs=in_specs,
            out_specs=[pl.BlockSpec((tq, 2 * hd), lambda b, p, i, pt: (b * nq + i, p)), row_spec],
            scratch_shapes=[pltpu.VMEM((2 * hd, seq), BF16), pltpu.VMEM((2 * hd, seq), BF16),
                            pltpu.VMEM((2, tq, LANES), F32), pltpu.VMEM((2, tq, 2 * hd), F32)]
            + _decode_scratch(heads, d),
        ),
        out_shape=[jax.ShapeDtypeStruct((t, d), BF16), jax.ShapeDtypeStruct((rows, 1, d), F32)],
        compiler_params=pltpu.CompilerParams(dimension_semantics=("parallel", "parallel", "arbitrary"),
                                             vmem_limit_bytes=FUSED_VMEM_LIMIT),
        name="fox_attn_decode",
    )(page_table.reshape(-1), q, kt, vt, ct, qd, k_new, v_new, f_new,
      *([cache_kt] * pages), *([cache_vt] * pages), *([cache_ft] * pages))


def _conv_kernel(u_ref, w_ref, b_ref, xs_ref, bm_ref, cm_ref, ext_ref, *, taps):
    tl = u_ref.shape[0]
    pad = 8

    @pl.when(pl.program_id(1) == 0)
    def _():
        ext_ref[0:pad, :] = jnp.zeros((pad, ext_ref.shape[1]), F32)

    @pl.when(pl.program_id(1) > 0)
    def _():
        ext_ref[0:pad, :] = ext_ref[tl:tl + pad, :]

    u = u_ref[...]
    ext_ref[pad:pad + tl, :] = u
    y = w_ref[0:1, :] * ext_ref[pl.ds(pad - (taps - 1), tl), :]
    for j in range(1, taps - 1):
        y = y + w_ref[j:j + 1, :] * ext_ref[pl.ds(pad - (taps - 1 - j), tl), :]
    y = y + w_ref[taps - 1:taps, :] * u + b_ref[...]
    y = y * _sigmoid(y)
    di, gn = xs_ref.shape[1], bm_ref.shape[1]
    xs_ref[...] = y[:, :di]
    bm_ref[...] = y[:, di:di + gn]
    cm_ref[...] = y[:, di + gn:]


def _ssd_conv(xbc, w, b, *, batch, di, gn, tl=256):
    t, cdim = xbc.shape
    seq = t // batch
    tl = min(tl, seq)
    nl = seq // tl
    taps = w.shape[0]
    return pl.pallas_call(
        functools.partial(_conv_kernel, taps=taps),
        grid=(batch, nl),
        in_specs=[
            pl.BlockSpec((tl, cdim), lambda bb, l: (bb * nl + l, 0)),
            pl.BlockSpec((taps, cdim), lambda bb, l: (0, 0)),
            pl.BlockSpec((1, cdim), lambda bb, l: (0, 0)),
        ],
        out_specs=[
            pl.BlockSpec((tl, di), lambda bb, l: (bb * nl + l, 0)),
            pl.BlockSpec((tl, gn), lambda bb, l: (bb * nl + l, 0)),
            pl.BlockSpec((tl, gn), lambda bb, l: (bb * nl + l, 0)),
        ],
        out_shape=[jax.ShapeDtypeStruct((t, di), F32), jax.ShapeDtypeStruct((t, gn), F32),
                   jax.ShapeDtypeStruct((t, gn), F32)],
        scratch_shapes=[pltpu.VMEM((tl + 8, cdim), F32)],
        compiler_params=_params("parallel", "arbitrary"),
        name="ssd_conv",
    )(xbc, w, b.reshape(1, cdim))


def _conv_step_kernel(u_ref, st_ref, w_ref, b_ref, o_ref):
    taps = w_ref.shape[0]
    y = w_ref[taps - 1:taps, :] * u_ref[...] + b_ref[...]
    for j in range(taps - 1):
        y = y + w_ref[j:j + 1, :] * st_ref[j]
    o_ref[...] = y * _sigmoid(y)


def _ssd_conv_step(u, state_t, w, b):
    rows, cdim = u.shape
    return pl.pallas_call(
        _conv_step_kernel,
        out_shape=jax.ShapeDtypeStruct((rows, cdim), F32),
        compiler_params=pltpu.CompilerParams(vmem_limit_bytes=VMEM_LIMIT),
        name="ssd_conv_step",
    )(u, state_t, w, b.reshape(1, cdim))


def _ssd_scan_kernel(xs_ref, dt_ref, bm_ref, cm_ref, a_ref, dsk_ref, y_ref, st_ref, ht_ref, *, groups, hd):
    c = pl.program_id(1)
    q, di = xs_ref.shape
    n = bm_ref.shape[1] // groups
    width = di // groups
    hpg = width // hd

    @pl.when(c == 0)
    def _():
        ht_ref[...] = jnp.zeros_like(ht_ref)

    ri = lax.broadcasted_iota(jnp.int32, (q, q), 0)
    ci = lax.broadcasted_iota(jnp.int32, (q, q), 1)
    causal = ci <= ri
    dt = dt_ref[...]
    acum = _onehot_dot(causal.astype(F32), dt * a_ref[...])
    acum_t = acum.T
    lo = lax.broadcasted_iota(jnp.int32, (q, 2 * hd), 1) < hd
    er = lax.broadcasted_iota(jnp.int32, (LANES, width), 0)
    ec = lax.broadcasted_iota(jnp.int32, (LANES, width), 1)
    for grp in range(groups):
        cols = slice(grp * width, (grp + 1) * width)
        ncols = slice(grp * n, (grp + 1) * n)
        expand = (er == grp * hpg + ec // hd).astype(F32)
        acum_x = _dot_onehot(acum, expand)
        dt_x = _dot_onehot(dt, expand)
        xs = xs_ref[:, cols]
        xdt = xs * dt_x
        xdt_b = xdt.astype(BF16)
        bm = bm_ref[:, ncols].astype(BF16)
        cm = cm_ref[:, ncols].astype(BF16)
        cb = lax.dot_general(cm, bm, NT_DIMS, preferred_element_type=F32)
        parts = []
        for pr in range(hpg // 2):
            xp = xdt_b[:, pr * 2 * hd:(pr + 1) * 2 * hd]
            ys = []
            for r in (grp * hpg + 2 * pr, grp * hpg + 2 * pr + 1):
                seg = acum[:, r:r + 1] - acum_t[r:r + 1, :]
                mat = (cb * jnp.exp(jnp.where(causal, seg, NEG))).astype(BF16)
                ys.append(jnp.dot(mat, xp, preferred_element_type=F32))
            parts.append(jnp.where(lo, ys[0], ys[1]))
        y_diag = jnp.concatenate(parts, axis=1)

        ht = ht_ref[:, cols]
        y_off = jnp.dot(cm, ht.astype(BF16), preferred_element_type=F32) * jnp.exp(acum_x)
        y_ref[:, cols] = y_diag + y_off + xs * dsk_ref[:, cols]

        a_last = acum_x[q - 1:q, :]
        xw = (xdt * jnp.exp(a_last - acum_x)).astype(BF16)
        st = jnp.dot(bm_ref[:, ncols].T.astype(BF16), xw, preferred_element_type=F32)
        ht_new = ht * jnp.exp(a_last) + st
        ht_ref[:, cols] = ht_new

        @pl.when(c == pl.num_programs(1) - 1)
        def _():
            st_ref[cols, :] = ht_new.T


def _ssd_scan(xs, dt, bm, cm, a_neg, d_skip, *, batch, groups, hd, chunk=SSD_CHUNK):
    t, di = xs.shape
    gn = bm.shape[1]
    n = gn // groups
    seq = t // batch
    nc = seq // chunk
    heads = di // hd
    a_row = jnp.zeros((1, LANES), F32).at[0, :heads].set(a_neg)
    dsk = jnp.repeat(d_skip, hd).reshape(1, di)
    rows = lambda width: pl.BlockSpec((chunk, width), lambda b, c: (b * nc + c, 0))
    const = lambda width: pl.BlockSpec((1, width), lambda b, c: (0, 0))
    return pl.pallas_call(
        functools.partial(_ssd_scan_kernel, groups=groups, hd=hd),
        grid=(batch, nc),
        in_specs=[rows(di), rows(LANES), rows(gn), rows(gn), const(LANES), const(di)],
        out_specs=[rows(di), pl.BlockSpec((None, di, n), lambda b, c: (b, 0, 0))],
        out_shape=[jax.ShapeDtypeStruct((t, di), F32), jax.ShapeDtypeStruct((batch, di, n), F32)],
        scratch_shapes=[pltpu.VMEM((n, di), F32)],
        compiler_params=_params("parallel", "arbitrary"),
        name="ssd_scan",
    )(xs, dt, bm, cm, a_row, dsk)


def _ssd_step_kernel(h_ref, xs_ref, dt_ref, a_ref, dsk_ref, bm_ref, cm_ref, y_ref, ho_ref, *, groups):
    rows = h_ref.shape[0]
    per = rows // groups
    xc, dtc = xs_ref[...], dt_ref[...]
    decay = jnp.exp(dtc * a_ref[...])
    xdt = xc * dtc
    for g in range(groups):
        sl = slice(g * per, (g + 1) * per)
        hn = h_ref[sl, :] * decay[sl] + xdt[sl] * bm_ref[g:g + 1, :]
        ho_ref[sl, :] = hn
        y_ref[sl, :] = (jnp.sum(hn * cm_ref[g:g + 1, :], axis=1, keepdims=True)
                        + xc[sl] * dsk_ref[sl, :])


def _ssd_step(h0, xs, dt, a_neg, d_skip, bm, cm, *, groups, hd):
    rows, di, n = h0.shape
    col = lambda v: v.reshape(rows, di, 1)
    a_col = jnp.repeat(a_neg, hd).reshape(di, 1)
    dsk = jnp.repeat(d_skip, hd).reshape(di, 1)
    row3 = lambda last: pl.BlockSpec((None, di, last), lambda b: (b, 0, 0))
    grp3 = pl.BlockSpec((None, groups, n), lambda b: (b, 0, 0))
    const = pl.BlockSpec((di, 1), lambda b: (0, 0))
    y, h_new = pl.pallas_call(
        functools.partial(_ssd_step_kernel, groups=groups),
        grid=(rows,),
        in_specs=[row3(n), row3(1), row3(1), const, const, grp3, grp3],
        out_specs=[row3(1), row3(n)],
        out_shape=[jax.ShapeDtypeStruct((rows, di, 1), F32), jax.ShapeDtypeStruct((rows, di, n), F32)],
        compiler_params=_params("parallel"),
        name="ssd_step",
    )(h0, col(xs), col(jnp.repeat(dt, hd, axis=1)), a_col, dsk,
      bm.reshape(rows, groups, n), cm.reshape(rows, groups, n))
    return y.reshape(rows, di), h_new


def _gated_out_kernel(y_ref, z_ref, ng_ref, w_ref, r_ref, o_ref, *, groups):
    z = z_ref[...]
    yg = y_ref[...] * (z * _sigmoid(z))
    per = yg.shape[1] // groups
    parts = []
    for g in range(groups):
        blk = yg[:, g * per:(g + 1) * per]
        parts.append(blk * lax.rsqrt(jnp.mean(blk * blk, axis=-1, keepdims=True) + EPS))
    yn = (jnp.concatenate(parts, axis=1) * ng_ref[...]).astype(BF16)
    o_ref[...] = r_ref[...] + jnp.dot(yn, w_ref[...], preferred_element_type=F32)


def _gated_out(y, z, ng, w, res, *, groups, tm=256):
    m, di = y.shape
    n = w.shape[1]
    tm = min(tm, m)
    return pl.pallas_call(
        functools.partial(_gated_out_kernel, groups=groups),
        grid=(m // tm,),
        in_specs=[
            pl.BlockSpec((tm, di), lambda i: (i, 0)),
            pl.BlockSpec((tm, di), lambda i: (i, 0)),
            pl.BlockSpec((1, di), lambda i: (0, 0)),
            pl.BlockSpec((di, n), lambda i: (0, 0)),
            pl.BlockSpec((tm, n), lambda i: (i, 0)),
        ],
        out_specs=pl.BlockSpec((tm, n), lambda i: (i, 0)),
        out_shape=jax.ShapeDtypeStruct((m, n), F32),
        compiler_params=_params("parallel"),
        name="ssd_gated_out",
    )(y, z, ng.reshape(1, di), w, res)


def _pool_kernel(x_ref, g_ref, w_ref, sc_ref, o_ref, pre_ref, ext_ref):
    tl, d = x_ref.shape
    pad = 16
    l = pl.program_id(1)

    @pl.when(l == 0)
    def _():
        ext_ref[0:pad, :] = jnp.zeros((pad, d), F32)

    @pl.when(l > 0)
    def _():
        ext_ref[0:pad, :] = ext_ref[tl:tl + pad, :]

    x = x_ref[...]
    xn = _rms(x, g_ref[...])
    ext_ref[pad:pad + tl, :] = xn
    pos = l * tl + lax.broadcasted_iota(jnp.int32, (tl, 1), 0)
    gd = d // len(POOL_WINDOWS)
    for gi, win in enumerate(POOL_WINDOWS):
        cols = slice(gi * gd, (gi + 1) * gd)
        tot = xn[:, cols]
        for j in range(1, win):
            tot = tot + ext_ref[pl.ds(pad - j, tl), cols]
        cnt = jnp.minimum(pos + 1, win).astype(F32)
        pooled = tot / cnt - xn[:, cols]
        mixed = jnp.dot(pooled.astype(BF16), w_ref[gi], preferred_element_type=F32)
        o_ref[:, cols] = x[:, cols] + mixed * sc_ref[:, cols]

    @pl.when(l == pl.num_programs(1) - 1)
    def _():
        pre_ref[...] = xn[tl - pad:, :]


def _pool(x, g, w_grp, scale, *, batch, tl=512):
    t, d = x.shape
    seq = t // batch
    tl = min(tl, seq)
    nl = seq // tl
    ng, gd = w_grp.shape[0], w_grp.shape[1]
    return pl.pallas_call(
        _pool_kernel,
        grid=(batch, nl),
        in_specs=[
            pl.BlockSpec((tl, d), lambda b, l: (b * nl + l, 0)),
            pl.BlockSpec((1, d), lambda b, l: (0, 0)),
            pl.BlockSpec((ng, gd, gd), lambda b, l: (0, 0, 0)),
            pl.BlockSpec((1, d), lambda b, l: (0, 0)),
        ],
        out_specs=[
            pl.BlockSpec((tl, d), lambda b, l: (b * nl + l, 0)),
            pl.BlockSpec((None, 16, d), lambda b, l: (b, 0, 0)),
        ],
        out_shape=[jax.ShapeDtypeStruct((t, d), F32), jax.ShapeDtypeStruct((batch, 16, d), F32)],
        scratch_shapes=[pltpu.VMEM((tl + 16, d), F32)],
        compiler_params=_params("parallel", "arbitrary"),
        name="pool",
    )(x, g.reshape(1, d), w_grp, scale.reshape(1, d))


def _pool_step_kernel(x_ref, st_ref, g_ref, w_ref, sc_ref, o_ref, xn_ref, *, pos):
    x = x_ref[...]
    xn = _rms(x, g_ref[...])
    xn_ref[...] = xn
    past = st_ref.shape[0]
    gd = x.shape[1] // len(POOL_WINDOWS)
    for gi, win in enumerate(POOL_WINDOWS):
        cols = slice(gi * gd, (gi + 1) * gd)
        tot = xn[:, cols]
        for j in range(1, win):
            tot = tot + st_ref[past - j][:, cols]
        pooled = tot / float(min(pos + 1, win)) - xn[:, cols]
        mixed = jnp.dot(pooled.astype(BF16), w_ref[gi], preferred_element_type=F32)
        o_ref[:, cols] = x[:, cols] + mixed * sc_ref[:, cols]


def _pool_step(x, state_t, g, w_grp, scale, *, pos):
    rows, d = x.shape
    return pl.pallas_call(
        functools.partial(_pool_step_kernel, pos=pos),
        out_shape=[jax.ShapeDtypeStruct((rows, d), F32), jax.ShapeDtypeStruct((rows, d), F32)],
        compiler_params=pltpu.CompilerParams(vmem_limit_bytes=VMEM_LIMIT),
        name="pool_step",
    )(x, state_t, g.reshape(1, d), w_grp, scale.reshape(1, d))


def _fox_proj_rows(h, g, w_qkv, w_f, b_f):
    d = h.shape[1]
    heads = w_f.shape[1]
    (qkv,) = _norm_linear(h, g, w_qkv.astype(BF16))
    wf = jnp.zeros((d, LANES), BF16).at[:, :heads].set(w_f.astype(BF16))
    bf = jnp.zeros((LANES,), F32).at[:heads].set(b_f)
    (logf,) = _norm_linear(h, g, wf, bf, act="log_sigmoid")
    return qkv[:, :d], qkv[:, d:2 * d], qkv[:, 2 * d:], logf[:, :heads]


def kernel(x_prompt, x_sample, cache_k, cache_v, cache_logf, page_table, state_ssm, state_conv, state_pool, norm_mix, norm_mlp, final_norm, fox_w_qkv, fox_w_f, fox_b_f, fox_w_o, ssd_w_in, ssd_conv_w, ssd_conv_b, ssd_dt_bias, ssd_a_log, ssd_d, ssd_norm, ssd_w_out, pool_w, pool_scale, mlp_w_up, mlp_w_down):
    bp, lp, d = x_prompt.shape
    bs, ls, _ = x_sample.shape
    assert ls == 1, "the sample group decodes one token per row"
    depth = norm_mix.shape[0]
    heads = fox_w_f.shape[-1]
    hd = d // heads
    n_layers, n_pool, psz = cache_k.shape[:3]
    past = page_table.shape[1] * psz
    ssd_heads = ssd_a_log.shape[1]
    di = ssd_norm.shape[1]
    cdim = ssd_conv_b.shape[1]
    n_state = state_ssm.shape[-1]
    ssd_hd = state_ssm.shape[-2]
    groups = (cdim - di) // (2 * n_state)
    gn = groups * n_state
    pool_past = state_pool.shape[2]

    hp = x_prompt.reshape(bp * lp, d)
    hs = x_sample.reshape(bs, d)
    cache_kt = jnp.transpose(cache_k, (0, 1, 3, 4, 2)).reshape(n_layers, n_pool, d, psz)
    cache_vt = jnp.transpose(cache_v, (0, 1, 3, 4, 2)).reshape(n_layers, n_pool, d, psz)
    cache_ft = jnp.swapaxes(cache_logf, 2, 3)
    stacked = None
    outs = {name: [] for name in ("fp", "ks", "vs", "fs", "ssm_p", "conv_p", "ssm_s",
                                  "conv_s", "pool_p", "pool_s")}
    i_fox = i_ssd = i_pool = 0
    for layer in range(depth):
        kind = layer % 3
        g = norm_mix[layer]
        if kind == 0:
            j = i_fox
            w_o = fox_w_o[j].astype(BF16)
            q, kt, kt16, vt, vt16, ft = _fox_proj(hp, g, fox_w_qkv[j], fox_w_f[j], fox_b_f[j], batch=bp,
                                                  layer=j, n_layers=n_layers, stacked=stacked)
            stacked = (kt, vt)
            ct = _cumsum_lanes(ft)
            qd, k, v, logf = _fox_proj_rows(hs, g, fox_w_qkv[j], fox_w_f[j], fox_b_f[j])
            dec_args = (qd.reshape(bs, 1, d), k.reshape(bs, 1, d), v.reshape(bs, 1, d),
                        logf.reshape(bs, heads, 1), cache_kt, cache_vt, cache_ft, page_table, j)
            tq = min(ATTN_BLOCK, lp)
            if bs == bp * (heads // 2) and page_table.shape[1] % (lp // tq) == 0:
                ctx, ctx_s = _fox_attn_decode(q, kt16, vt16, ct, *dec_args, batch=bp, tq=tq)
            else:
                ctx = _fox_attn(q, kt16, vt16, ct, batch=bp, tq=tq)
                ctx_s = _fox_decode(*dec_args, heads=heads)
            hp = _linear_res(ctx, w_o, hp)
            outs["fp"].append(jnp.swapaxes(ft, 1, 2))
            hs = _linear_res(ctx_s.reshape(bs, d).astype(BF16), w_o, hs)
            outs["ks"].append(k.reshape(bs, 1, heads, hd))
            outs["vs"].append(v.reshape(bs, 1, heads, hd))
            outs["fs"].append(logf.reshape(bs, 1, heads))
            i_fox += 1
        elif kind == 1:
            j = i_ssd
            w_in = ssd_w_in[j]
            w_z = w_in[:, :di].astype(BF16)
            w_xbc = w_in[:, di:di + cdim].astype(BF16)
            w_dt = jnp.zeros((d, LANES), BF16).at[:, :ssd_heads].set(w_in[:, di + cdim:].astype(BF16))
            b_dt = jnp.zeros((LANES,), F32).at[:ssd_heads].set(ssd_dt_bias[j])
            a_neg = -jnp.exp(ssd_a_log[j])
            w_out = ssd_w_out[j].astype(BF16)

            def in_proj(h):
                (z,) = _norm_linear(h, g, w_z)
                (xbc,) = _norm_linear(h, g, w_xbc)
                (dt,) = _norm_linear(h, g, w_dt, b_dt, act="softplus")
                return z, xbc, dt

            z, xbc, dt = in_proj(hp)
            xs, bm, cm = _ssd_conv(xbc, ssd_conv_w[j], ssd_conv_b[j], batch=bp, di=di, gn=gn)
            y, h_fin = _ssd_scan(xs, dt, bm, cm, a_neg, ssd_d[j], batch=bp, groups=groups, hd=ssd_hd)
            hp = _gated_out(y, z, ssd_norm[j], w_out, hp, groups=groups)
            outs["ssm_p"].append(h_fin.reshape(bp, ssd_heads, ssd_hd, n_state))
            outs["conv_p"].append(xbc.reshape(bp, lp, cdim)[:, lp - (ssd_conv_w.shape[1] - 1):])

            z, xbc, dt = in_proj(hs)
            conv_state = state_conv[j]
            xc = _ssd_conv_step(xbc, jnp.swapaxes(conv_state, 0, 1), ssd_conv_w[j], ssd_conv_b[j])
            y, h_new = _ssd_step(state_ssm[j].reshape(bs, di, n_state), xc[:, :di], dt[:, :ssd_heads],
                                 a_neg, ssd_d[j], xc[:, di:di + gn], xc[:, di + gn:],
                                 groups=groups, hd=ssd_hd)
            hs = _gated_out(y, z, ssd_norm[j], w_out, hs, groups=groups)
            outs["ssm_s"].append(h_new.reshape(bs, ssd_heads, ssd_hd, n_state))
            outs["conv_s"].append(jnp.concatenate([conv_state[:, 1:], xbc[:, None]], axis=1))
            i_ssd += 1
        else:
            j = i_pool
            w_grp = pool_w[j].astype(BF16)
            hp, pre = _pool(hp, g, w_grp, pool_scale[j], batch=bp)
            outs["pool_p"].append(pre[:, 16 - pool_past:])
            pool_state = state_pool[j]
            hs, xn_s = _pool_step(hs, jnp.swapaxes(pool_state, 0, 1), g, w_grp, pool_scale[j], pos=past)
            outs["pool_s"].append(jnp.concatenate([pool_state[:, 1:], xn_s[:, None]], axis=1))
            i_pool += 1
        w_up, w_down = mlp_w_up[layer].astype(BF16), mlp_w_down[layer].astype(BF16)
        final_g = final_norm if layer == depth - 1 else None
        hp = _mlp(hp, norm_mlp[layer], w_up, w_down, final_g)
        hs = _mlp(hs, norm_mlp[layer], w_up, w_down, final_g)
    y_prompt = hp.reshape(bp, lp, d)
    y_sample = hs.reshape(bs, ls, d)
    k_prompt, v_prompt = (jnp.transpose(a.reshape(n_layers, bp, heads, hd, lp), (0, 1, 4, 2, 3))
                          for a in stacked)
    return (y_prompt, y_sample, k_prompt, v_prompt) + tuple(jnp.stack(outs[name]) for name in (
        "fp", "ks", "vs", "fs", "ssm_p", "conv_p", "ssm_s", "conv_s", "pool_p", "pool_s"))
```
